```python
import math
import jax, jax.numpy as jnp
from jax import lax
import numpy as np

D_MODEL = 1024
BATCH = 16
SEQ = 2048
DEPTH = 1
DEC_BATCH = 32
DEC_SEQ = 32
PAST_LEN = 4096

CHUNK = 64
QBLK = 128
PE_DIM = 256
DA_HEADS = 4
DA_HD = 64
DA_VD = 2 * DA_HD
RT_HEADS = 4
RT_KD = 128
RT_VD = 128
D_FF = 2816
CONV_W = 3
N_BUCKETS = 32
MAX_DIST = 128
EPS = 1e-6
ROPE_BASE = 10000.0

DA_Q = DA_HEADS * 2 * DA_HD
DA_K = DA_HEADS * 2 * DA_HD
DA_V = DA_HEADS * DA_VD
RT_Q = RT_HEADS * RT_KD
RT_K = RT_HEADS * RT_KD
RT_V = RT_HEADS * RT_VD
RT_G = RT_HEADS * RT_VD
IN_WIDTHS = (DA_Q, DA_K, DA_V, RT_Q, RT_K, RT_V, RT_G, D_MODEL, D_MODEL)
D_IN = DA_Q + DA_K + DA_V + RT_Q + RT_K + RT_V + RT_G + 2 * D_MODEL

kernel_name = "diffattn_retention_parallel_streaming_encoder"


def rmsnorm(x, g):
    xf = x.astype(jnp.float32)
    y = xf * lax.rsqrt(jnp.mean(xf * xf, axis=-1, keepdims=True) + EPS)
    return (y * g.astype(jnp.float32)).astype(x.dtype)


def t5_bucket(rel):
    nb = N_BUCKETS // 2
    ret = jnp.where(rel > 0, nb, 0)
    n = jnp.abs(rel)
    max_exact = nb // 2
    nf = jnp.maximum(n, 1).astype(jnp.float32)
    large = max_exact + (jnp.log(nf / max_exact) / math.log(MAX_DIST / max_exact) * (nb - max_exact)).astype(jnp.int32)
    large = jnp.minimum(large, nb - 1)
    return ret + jnp.where(n < max_exact, n, large)


def diff_attn_block(q1, q2, qpos, k1, k2, v, kpos, rel_bias, lam):
    bias = jnp.transpose(rel_bias[t5_bucket(kpos[None, :] - qpos[:, None])], (2, 0, 1)).astype(jnp.float32)
    mask = (kpos[None, :] // CHUNK) <= (qpos[:, None] // CHUNK)

    def probs(q, k):
        s = jnp.einsum('bqhd,bkhd->bhqk', q, k).astype(jnp.float32) + bias
        s = jnp.where(mask, s, -jnp.inf)
        return jax.nn.softmax(s, axis=-1)

    a = probs(q1, k1) - lam * probs(q2, k2)
    return jnp.einsum('bhqk,bkhe->bqhe', a.astype(v.dtype), v)


def diff_attention(q1, q2, qpos, k1, k2, v, kpos, rel_bias, lam):
    B, T, H, _ = q1.shape
    blk = min(QBLK, T)
    nb = T // blk

    def to_blocks(x):
        return x.reshape((B, nb, blk) + x.shape[2:]).swapaxes(0, 1)

    def one(args):
        bq1, bq2, bqp = args
        return diff_attn_block(bq1, bq2, bqp, k1, k2, v, kpos, rel_bias, lam)

    o = lax.map(one, (to_blocks(q1), to_blocks(q2), qpos.reshape(nb, blk)))
    return o.swapaxes(0, 1).reshape(B, T, H, v.shape[-1])


def rotary(x, pos):
    half = x.shape[-1] // 2
    inv = jnp.power(ROPE_BASE, -jnp.arange(half, dtype=jnp.float32) / half)
    ang = pos.astype(jnp.float32)[:, None] * inv[None, :]
    cos = jnp.cos(ang)[None, :, None, :]
    sin = jnp.sin(ang)[None, :, None, :]
    xf = x.astype(jnp.float32)
    x1, x2 = xf[..., :half], xf[..., half:]
    return jnp.concatenate([x1 * cos - x2 * sin, x1 * sin + x2 * cos], axis=-1).astype(x.dtype)


def retention(q, k, v, s0):
    B, T, H, dk = q.shape
    dv = v.shape[-1]
    c = min(CHUNK, T)
    n = T // c
    f32 = jnp.float32
    log_g = jnp.log1p(-jnp.power(2.0, -5.0 - jnp.arange(H, dtype=f32)))
    idx = jnp.arange(c, dtype=f32)
    diff = idx[:, None] - idx[None, :]
    dec_intra = jnp.where(diff >= 0, jnp.exp(log_g[:, None, None] * jnp.maximum(diff, 0.0)), 0.0)
    dec_q = jnp.exp(log_g[None, :] * (idx + 1.0)[:, None])
    dec_k = jnp.exp(log_g[None, :] * (c - 1.0 - idx)[:, None])
    dec_c = jnp.exp(log_g * c)

    def to_chunks(x):
        return x.astype(f32).reshape((B, n, c) + x.shape[2:]).swapaxes(0, 1)

    def step(S, inp):
        qi, ki, vi = inp
        att = jnp.einsum('bqhd,bkhd->bhqk', qi, ki) * dec_intra
        o = jnp.einsum('bhqk,bkhe->bqhe', att, vi)
        o = o + jnp.einsum('bqhd,bhde->bqhe', qi, S) * dec_q[None, :, :, None]
        S = S * dec_c[None, :, None, None] + jnp.einsum('bkhd,bkhe->bhde', ki * dec_k[None, :, :, None], vi)
        return S, o

    S, o = lax.scan(step, s0.astype(f32), (to_chunks(q), to_chunks(k), to_chunks(v)))
    return o.swapaxes(0, 1).reshape(B, T, H, dv), S.astype(s0.dtype)


def conv_ffn(f, s_conv, w_g, w_u, conv_w, conv_b, w_d):
    T = f.shape[1]
    g = f @ w_g
    u = f @ w_u
    gp = jnp.concatenate([s_conv.astype(g.dtype), g], axis=1)
    gc = conv_b + gp[:, 0:T] * conv_w[0]
    for j in range(1, CONV_W):
        gc = gc + gp[:, j:j + T] * conv_w[j]
    y = (jax.nn.gelu(gc) * u) @ w_d
    return y, gp[:, -(CONV_W - 1):]


def _layer(h, pe, cache_k, cache_v, s_ret, s_conv, rel_bias, lam_init,
           g_mix, w_in, g_q, g_k, lam_q1, lam_k1, lam_q2, lam_k2, g_da, g_rt,
           w_bd, w_br, w_o, g_ffn, w_g, w_u, conv_w, conv_b, w_d, g_pe, w_pe, w_pg):
    B, T, _ = h.shape
    past = 0 if cache_k is None else cache_k.shape[1]
    qpos = past + jnp.arange(T, dtype=jnp.int32)
    kpos = jnp.arange(past + T, dtype=jnp.int32)

    a = rmsnorm(h, g_mix)
    z = a @ w_in
    pts = np.cumsum(IN_WIDTHS)[:-1].tolist()
    qd, kd, vd, qr, kr, vr, gr, gate_d, gate_r = jnp.split(z, pts, axis=-1)

    q = rmsnorm(qd.reshape(B, T, DA_HEADS, 2, DA_HD), g_q) * (DA_HD ** -0.5)
    k = rmsnorm(kd.reshape(B, T, DA_HEADS, 2, DA_HD), g_k)
    v = vd.reshape(B, T, DA_HEADS, DA_VD)
    k_rows = k.reshape(B, T, DA_HEADS, 2 * DA_HD)
    if cache_k is None:
        k_all, v_all = k_rows, v
    else:
        k_all = jnp.concatenate([cache_k.astype(k_rows.dtype), k_rows], axis=1)
        v_all = jnp.concatenate([cache_v.astype(v.dtype), v], axis=1)
    lam = (jnp.exp(jnp.sum(lam_q1.astype(jnp.float32) * lam_k1.astype(jnp.float32)))
           - jnp.exp(jnp.sum(lam_q2.astype(jnp.float32) * lam_k2.astype(jnp.float32))) + lam_init)
    o_d = diff_attention(q[..., 0, :], q[..., 1, :], qpos, k_all[..., :DA_HD], k_all[..., DA_HD:],
                         v_all, kpos, rel_bias, lam)
    o_d = (rmsnorm(o_d, g_da) * (1.0 - lam_init)).reshape(B, T, DA_V)

    qr = rotary(qr.reshape(B, T, RT_HEADS, RT_KD), qpos)
    kr = rotary(kr.reshape(B, T, RT_HEADS, RT_KD), qpos) * (RT_KD ** -0.5)
    vr = vr.reshape(B, T, RT_HEADS, RT_VD)
    s0 = jnp.zeros((B, RT_HEADS, RT_KD, RT_VD), jnp.float32) if s_ret is None else s_ret
    o_r, s_new = retention(qr, kr, vr, s0)
    o_r = rmsnorm(o_r.astype(h.dtype), g_rt).reshape(B, T, RT_V) * jax.nn.silu(gr)

    mix = jax.nn.sigmoid(gate_d) * (o_d @ w_bd) + jax.nn.sigmoid(gate_r) * (o_r @ w_br)
    h = h + mix @ w_o

    sc = jnp.zeros((B, CONV_W - 1, D_FF), h.dtype) if s_conv is None else s_conv
    ffn_out, conv_new = conv_ffn(rmsnorm(h, g_ffn), sc, w_g, w_u, conv_w, conv_b, w_d)
    h = h + ffn_out

    h = h + (pe.astype(h.dtype) @ w_pe) * jax.nn.sigmoid(rmsnorm(h, g_pe) @ w_pg)
    return h, k_rows, v, s_new, conv_new


def setup_inputs(seed: int = 0) -> dict:
    key = jax.random.key(seed)
    ks = jax.random.split(key, 40)

    def nrm(k, shape, scale):
        return jax.random.normal(k, shape, jnp.float32) * scale

    def gain(k, shape):
        return 1.0 + 0.01 * jax.random.normal(k, shape, jnp.float32)

    L = DEPTH
    return {
        "x_prompt": nrm(ks[0], (BATCH, SEQ, D_MODEL), 1.0),
        "x_sample": nrm(ks[1], (DEC_BATCH, DEC_SEQ, D_MODEL), 1.0),
        "p_prompt": nrm(ks[2], (L, BATCH, SEQ, PE_DIM), 1.0),
        "p_sample": nrm(ks[3], (L, DEC_BATCH, DEC_SEQ, PE_DIM), 1.0),
        "cache_k": nrm(ks[4], (L, DEC_BATCH, PAST_LEN, DA_HEADS, 2 * DA_HD), 1.0),
        "cache_v": nrm(ks[5], (L, DEC_BATCH, PAST_LEN, DA_HEADS, DA_VD), 1.0),
        "state_ret": nrm(ks[6], (L, DEC_BATCH, RT_HEADS, RT_KD, RT_VD), 0.3),
        "state_conv": nrm(ks[7], (L, DEC_BATCH, CONV_W - 1, D_FF), 1.0),
        "rel_bias": nrm(ks[8], (N_BUCKETS, DA_HEADS), 0.5),
        "g_mix": gain(ks[9], (L, D_MODEL)),
        "w_in": nrm(ks[10], (L, D_MODEL, D_IN), D_MODEL ** -0.5),
        "g_q": gain(ks[11], (L, DA_HD)),
        "g_k": gain(ks[12], (L, DA_HD)),
        "lam_q1": nrm(ks[13], (L, DA_HD), 0.1),
        "lam_k1": nrm(ks[14], (L, DA_HD), 0.1),
        "lam_q2": nrm(ks[15], (L, DA_HD), 0.1),
        "lam_k2": nrm(ks[16], (L, DA_HD), 0.1),
        "g_da": gain(ks[17], (L, DA_VD)),
        "g_rt": gain(ks[18], (L, RT_VD)),
        "w_bd": nrm(ks[19], (L, DA_V, D_MODEL), DA_V ** -0.5),
        "w_br": nrm(ks[20], (L, RT_V, D_MODEL), RT_V ** -0.5),
        "w_o": nrm(ks[21], (L, D_MODEL, D_MODEL), D_MODEL ** -0.5),
        "g_ffn": gain(ks[22], (L, D_MODEL)),
        "w_g": nrm(ks[23], (L, D_MODEL, D_FF), D_MODEL ** -0.5),
        "w_u": nrm(ks[24], (L, D_MODEL, D_FF), D_MODEL ** -0.5),
        "conv_w": nrm(ks[25], (L, CONV_W, D_FF), CONV_W ** -0.5),
        "conv_b": nrm(ks[26], (L, D_FF), 0.01),
        "w_d": nrm(ks[27], (L, D_FF, D_MODEL), D_FF ** -0.5),
        "g_pe": gain(ks[28], (L, D_MODEL)),
        "w_pe": nrm(ks[29], (L, PE_DIM, D_MODEL), PE_DIM ** -0.5),
        "w_pg": nrm(ks[30], (L, D_MODEL, D_MODEL), D_MODEL ** -0.5),
    }


def reference(x_prompt, x_sample, p_prompt, p_sample, cache_k, cache_v, state_ret, state_conv,
              rel_bias, g_mix, w_in, g_q, g_k, lam_q1, lam_k1, lam_q2, lam_k2, g_da, g_rt,
              w_bd, w_br, w_o, g_ffn, w_g, w_u, conv_w, conv_b, w_d, g_pe, w_pe, w_pg):
    hp, hs = x_prompt, x_sample
    kps, vps, rps, cps = [], [], [], []
    kss, vss, rss, css = [], [], [], []
    for i in range(DEPTH):
        lam_init = 0.8 - 0.6 * math.exp(-0.3 * i)
        w = (g_mix[i], w_in[i], g_q[i], g_k[i], lam_q1[i], lam_k1[i], lam_q2[i], lam_k2[i],
             g_da[i], g_rt[i], w_bd[i], w_br[i], w_o[i], g_ffn[i], w_g[i], w_u[i],
             conv_w[i], conv_b[i], w_d[i], g_pe[i], w_pe[i], w_pg[i])
        hp, kp, vp, rp, cp = _layer(hp, p_prompt[i], None, None, None, None, rel_bias, lam_init, *w)
        hs, ksm, vsm, rsm, csm = _layer(hs, p_sample[i], cache_k[i], cache_v[i], state_ret[i],
                                        state_conv[i], rel_bias, lam_init, *w)
        kps.append(kp); vps.append(vp); rps.append(rp); cps.append(cp)
        kss.append(ksm); vss.append(vsm); rss.append(rsm); css.append(csm)
    k_prompt = jnp.stack(kps)
    v_prompt = jnp.stack(vps)
    ret_prompt = jnp.stack(rps)
    conv_prompt = jnp.stack(cps)
    k_sample = jnp.stack(kss)
    v_sample = jnp.stack(vss)
    ret_sample = jnp.stack(rss)
    conv_sample = jnp.stack(css)
    return (hp, hs, k_prompt, v_prompt, ret_prompt, conv_prompt, k_sample, v_sample, ret_sample, conv_sample)
```

```python
import functools
import math

import numpy as np
import jax
import jax.numpy as jnp
from jax import lax
from jax.experimental import pallas as pl
from jax.experimental.pallas import tpu as pltpu

D_MODEL = 1024
CHUNK = 64
PE_DIM = 256
HEADS = 4
HEAD_W = 128
DA_HD = 64
RT_KD = 128
D_FF = 2816
CONV_W = 3
N_BUCKETS = 32
MAX_DIST = 128
EPS = 1e-6
ROPE_BASE = 10000.0
SEG = HEADS * HEAD_W
D_IN = 7 * SEG + 2 * D_MODEL

MASK_VALUE = -1e30
V7X_VMEM_LIMIT = 56 * 1024 * 1024
ATTN_BLOCK = 256
NEAR_WINDOW = 256
F32 = jnp.float32
BF16 = jnp.bfloat16


def _cparams(*sem):
    return pltpu.CompilerParams(dimension_semantics=sem, vmem_limit_bytes=V7X_VMEM_LIMIT)


def _const_spec(shape):
    nd = len(shape)
    return pl.BlockSpec(shape, lambda *_: (0,) * nd, pipeline_mode=pl.Buffered(1))


def _rms(x, g):
    return x * lax.rsqrt(jnp.mean(x * x, axis=-1, keepdims=True) + EPS) * g


def _t5_bucket(rel):
    nb = N_BUCKETS // 2
    ret = jnp.where(rel > 0, nb, 0)
    n = jnp.abs(rel)
    max_exact = nb // 2
    nf = jnp.maximum(n, 1).astype(F32)
    large = max_exact + (jnp.log(nf / max_exact) / math.log(MAX_DIST / max_exact) * (nb - max_exact)).astype(jnp.int32)
    large = jnp.minimum(large, nb - 1)
    return ret + jnp.where(n < max_exact, n, large)


def _bias_kernel(idx_ref, madd_ref, rb_ref, out_ref):
    h = pl.program_id(0)
    idx = idx_ref[...]
    bias = jnp.zeros(idx.shape, F32)
    for b in range(N_BUCKETS):
        bias = jnp.where(idx == b, rb_ref[b, h], bias)
    out_ref[0] = bias + madd_ref[...]


def _bias_tiles(rel_bias, qpos, kpos):
    rel = kpos[:, None, :] - qpos[:, :, None]
    idx = _t5_bucket(rel).astype(jnp.int32)
    madd = jnp.where((kpos[:, None, :] // CHUNK) <= (qpos[:, :, None] // CHUNK), 0.0, MASK_VALUE).astype(F32)
    n, r, c = idx.shape
    return pl.pallas_call(
        _bias_kernel,
        grid=(HEADS,),
        in_specs=[pl.BlockSpec((n, r, c), lambda h: (0, 0, 0)),
                  pl.BlockSpec((n, r, c), lambda h: (0, 0, 0)),
                  pl.BlockSpec(memory_space=pltpu.SMEM)],
        out_specs=pl.BlockSpec((1, n, r, c), lambda h: (h, 0, 0, 0)),
        out_shape=jax.ShapeDtypeStruct((HEADS, n, r, c), F32),
        compiler_params=_cparams("arbitrary"),
        name="bias_tiles",
    )(idx, madd, rel_bias)


def _inproj_kernel(x_ref, cos_ref, sin_ref, gmix_ref, w_ref, gq_ref, gk_ref, gmat_ref,
                   q_ref, k32_ref, kb_ref, v32_ref, vb_ref, qr_ref, kr_ref, vr_ref, sg_ref, sd_ref, sr_ref):
    a = _rms(x_ref[...], gmix_ref[...]).astype(BF16)

    def seg(c0, c1):
        return jnp.dot(a, w_ref[:, c0:c1], preferred_element_type=F32)

    def group_norm(z, g):
        sq = (z * z).astype(BF16)
        outs = []
        for c in range(0, SEG, 256):
            ms = jnp.dot(sq[:, c:c + 256], gmat_ref[...], preferred_element_type=F32)
            outs.append(z[:, c:c + 256] * lax.rsqrt(ms + EPS))
        return jnp.concatenate(outs, axis=1) * g

    def rotary(z):
        cos, sin = cos_ref[...], sin_ref[...]
        outs = []
        for h in range(HEADS):
            zs = z[:, h * HEAD_W:(h + 1) * HEAD_W]
            outs.append(zs * cos + pltpu.roll(zs, HEAD_W // 2, 1) * sin)
        return jnp.concatenate(outs, axis=1)

    q_ref[...] = (group_norm(seg(0, SEG), gq_ref[...]) * DA_HD ** -0.5).astype(BF16)
    k = group_norm(seg(SEG, 2 * SEG), gk_ref[...])
    k32_ref[...] = k
    kb_ref[...] = k.astype(BF16)
    v = seg(2 * SEG, 3 * SEG)
    v32_ref[...] = v
    vb_ref[...] = v.astype(BF16)
    qr_ref[...] = rotary(seg(3 * SEG, 4 * SEG)).astype(BF16)
    kr_ref[...] = (rotary(seg(4 * SEG, 5 * SEG)) * RT_KD ** -0.5).astype(BF16)
    vr_ref[...] = seg(5 * SEG, 6 * SEG).astype(BF16)
    gr = seg(6 * SEG, 7 * SEG)
    sg_ref[...] = (gr * jax.nn.sigmoid(gr)).astype(BF16)
    sd_ref[...] = jax.nn.sigmoid(seg(7 * SEG, 7 * SEG + D_MODEL)).astype(BF16)
    sr_ref[...] = jax.nn.sigmoid(seg(7 * SEG + D_MODEL, D_IN)).astype(BF16)


def _inproj(x2, cos_t, sin_t, gmix, w_in, gq, gk, gmat, tm):
    n = x2.shape[0]
    nt = cos_t.shape[0] // tm
    row = lambda w: pl.BlockSpec((tm, w), lambda i: (i, 0))
    tab = pl.BlockSpec((tm, HEAD_W), lambda i: (i % nt, 0))
    out_w = [(SEG, BF16), (SEG, F32), (SEG, BF16), (SEG, F32), (SEG, BF16), (SEG, BF16), (SEG, BF16),
             (SEG, BF16), (SEG, BF16), (D_MODEL, BF16), (D_MODEL, BF16)]
    return pl.pallas_call(
        _inproj_kernel,
        grid=(n // tm,),
        in_specs=[row(D_MODEL), tab, tab, _const_spec((1, D_MODEL)), _const_spec((D_MODEL, D_IN)),
                  _const_spec((1, SEG)), _const_spec((1, SEG)), _const_spec((256, 256))],
        out_specs=[row(w) for w, _ in out_w],
        out_shape=[jax.ShapeDtypeStruct((n, w), dt) for w, dt in out_w],
        compiler_params=_cparams("arbitrary"),
        name="inproj",
    )(x2, cos_t, sin_t, gmix, w_in, gq, gk, gmat)


def _split_maps(q):
    lane = lax.broadcasted_iota(jnp.int32, q.shape, 1)
    zero = jnp.zeros_like(q)
    return jnp.concatenate([jnp.where(lane < DA_HD, q, zero), jnp.where(lane >= DA_HD, q, zero)], axis=0)


def _flash_init(m_scr, l_scr, acc_scr):
    m_scr[...] = jnp.full(m_scr.shape, MASK_VALUE, F32)
    l_scr[...] = jnp.zeros(l_scr.shape, F32)
    acc_scr[...] = jnp.zeros(acc_scr.shape, F32)


def _flash_step(qq, k, v, bias, m_scr, l_scr, acc_scr):
    s = lax.dot_general(qq, k, (((1,), (1,)), ((), ())), preferred_element_type=F32) + bias
    m_prev = m_scr[...]
    m_new = jnp.maximum(m_prev, jnp.max(s, axis=1, keepdims=True))
    alpha = jnp.exp(m_prev - m_new)
    p = jnp.exp(s - m_new[:, :1])
    l_scr[...] = alpha * l_scr[...] + jnp.sum(p, axis=1, keepdims=True)
    acc_scr[...] = alpha * acc_scr[...] + jnp.dot(p.astype(BF16), v, preferred_element_type=F32)
    m_scr[...] = m_new


def _lam(lamv_ref, lam_init):
    lv = lamv_ref[...]
    s1 = jnp.sum(lv[0:1] * lv[1:2], axis=1, keepdims=True)
    s2 = jnp.sum(lv[2:3] * lv[3:4], axis=1, keepdims=True)
    return jnp.exp(s1) - jnp.exp(s2) + lam_init


def _flash_finish(t, lam, lam_init, gda_ref, l_scr, acc_scr):
    o = acc_scr[...] / l_scr[...]
    o = o[:t] - lam * o[t:]
    return _rms(o, gda_ref[...]) * (1.0 - lam_init)


def _retention_step(qr, kr, vr, s_prev, dmat, dq, dk, dc):
    att = lax.dot_general(qr, kr, (((1,), (1,)), ((), ())), preferred_element_type=F32) * dmat
    o = jnp.dot(att.astype(BF16), vr, preferred_element_type=F32)
    o = o + jnp.dot(qr, s_prev.astype(BF16), preferred_element_type=F32) * dq
    kdec = (kr.astype(F32) * dk).astype(BF16)
    s_new = s_prev * dc + lax.dot_general(kdec, vr, (((0,), (0,)), ((), ())), preferred_element_type=F32)
    return o, s_new


def _decay_tables(c):
    log_g = jnp.log1p(-jnp.power(2.0, -5.0 - jnp.arange(HEADS, dtype=F32)))
    idx = jnp.arange(c, dtype=F32)
    diff = idx[:, None] - idx[None, :]
    dmat = jnp.where(diff >= 0, jnp.exp(log_g[:, None, None] * jnp.maximum(diff, 0.0)), 0.0)
    dq = jnp.broadcast_to(jnp.exp(log_g[:, None] * (idx + 1.0)[None, :])[:, :, None], (HEADS, c, HEAD_W))
    dk = jnp.broadcast_to(jnp.exp(log_g[:, None] * (c - 1.0 - idx)[None, :])[:, :, None], (HEADS, c, HEAD_W))
    dc = jnp.broadcast_to(jnp.exp(log_g * c)[:, None, None], (HEADS, 8, HEAD_W))
    return dmat.astype(F32), dq.astype(F32), dk.astype(F32), dc.astype(F32)


def _mixer_prompt_kernel(lam_init, q_ref, k_ref, v_ref, bias_ref, qr_ref, kr_ref, vr_ref, sg_ref,
                         dmat_ref, dq_ref, dk_ref, dc_ref, s0_ref, lamv_ref, gda_ref, grt_ref,
                         od_ref, or_ref, sout_ref, m_scr, l_scr, acc_scr, s_scr):
    qb = pl.program_id(2)
    t = q_ref.shape[0]
    qq = _split_maps(q_ref[...])
    _flash_init(m_scr, l_scr, acc_scr)

    def body(kb, carry):
        off = pl.multiple_of(kb * t, t)
        bias = bias_ref[0, jnp.minimum(qb - kb, 2)]
        _flash_step(qq, k_ref[pl.ds(off, t), :], v_ref[pl.ds(off, t), :],
                    jnp.concatenate([bias, bias], axis=0), m_scr, l_scr, acc_scr)
        return carry

    lax.fori_loop(0, qb + 1, body, 0)
    od_ref[...] = _flash_finish(t, _lam(lamv_ref, lam_init), lam_init, gda_ref, l_scr, acc_scr).astype(BF16)

    @pl.when(qb == 0)
    def _():
        s_scr[...] = s0_ref[0, 0]

    o, s_new = _retention_step(qr_ref[...], kr_ref[...], vr_ref[...], s_scr[...],
                               dmat_ref[0], dq_ref[0], dk_ref[0], dc_ref[0, 0:1])
    s_scr[...] = s_new
    sout_ref[0, 0] = s_new
    or_ref[...] = (_rms(o, grt_ref[...]) * sg_ref[...].astype(F32)).astype(BF16)


def _mixer_prompt(lam_init, b, t, q, kb, vb, bias, qr, kr, vr, sg, s0, lamv, gda, grt):
    tq = ATTN_BLOCK
    nq = t // tq
    n = b * t
    dmat, dq, dk, dc = _decay_tables(tq)
    tile = pl.BlockSpec((tq, HEAD_W), lambda bi, h, qb: (bi * nq + qb, h))
    seq = pl.BlockSpec((t, HEAD_W), lambda bi, h, qb: (bi, h))
    per_head = lambda *s: pl.BlockSpec((1,) + s, lambda bi, h, qb: (h,) + (0,) * len(s))
    state = pl.BlockSpec((1, 1, RT_KD, HEAD_W), lambda bi, h, qb: (bi, h, 0, 0))
    vec = lambda w: pl.BlockSpec((1, w), lambda bi, h, qb: (0, 0))
    return pl.pallas_call(
        functools.partial(_mixer_prompt_kernel, lam_init),
        grid=(b, HEADS, nq),
        in_specs=[tile, seq, seq, per_head(3, tq, tq), tile, tile, tile, tile,
                  per_head(tq, tq), per_head(tq, HEAD_W), per_head(tq, HEAD_W), per_head(8, HEAD_W),
                  state, pl.BlockSpec((4, DA_HD), lambda bi, h, qb: (0, 0)), vec(HEAD_W), vec(HEAD_W)],
        out_specs=[tile, tile, state],
        out_shape=[jax.ShapeDtypeStruct((n, SEG), BF16), jax.ShapeDtypeStruct((n, SEG), BF16),
                   jax.ShapeDtypeStruct((b, HEADS, RT_KD, HEAD_W), F32)],
        scratch_shapes=[pltpu.VMEM((2 * tq, HEAD_W), F32), pltpu.VMEM((2 * tq, HEAD_W), F32),
                        pltpu.VMEM((2 * tq, HEAD_W), F32), pltpu.VMEM((RT_KD, HEAD_W), F32)],
        compiler_params=_cparams("arbitrary", "arbitrary", "arbitrary"),
        name="mixer_prompt",
    )(q, kb, vb, bias, qr, kr, vr, sg, dmat, dq, dk, dc, s0, lamv, gda, grt)


def _mixer_sample_kernel(lam_init, kc, q_ref, ck_ref, cv_ref, kn_ref, vn_ref, bnear_ref, bnew_ref, rb_ref,
                         qr_ref, kr_ref, vr_ref, sg_ref, dmat_ref, dq_ref, dk_ref, dc_ref, s0_ref,
                         lamv_ref, gda_ref, grt_ref, od_ref, or_ref, sout_ref, m_scr, l_scr, acc_scr):
    h = pl.program_id(1)
    t = q_ref.shape[0]
    past = ck_ref.shape[1]
    n_far = (past - NEAR_WINDOW) // kc
    qq = _split_maps(q_ref[...])
    _flash_init(m_scr, l_scr, acc_scr)
    far_bias = rb_ref[N_BUCKETS // 2 - 1, h]

    def body(c, carry):
        off = pl.multiple_of(c * kc, kc)
        _flash_step(qq, ck_ref[0, pl.ds(off, kc), :].astype(BF16), cv_ref[0, pl.ds(off, kc), :].astype(BF16),
                    far_bias, m_scr, l_scr, acc_scr)
        return carry

    lax.fori_loop(0, n_far, body, 0)
    near = past - NEAR_WINDOW
    bnear = bnear_ref[0, 0]
    _flash_step(qq, ck_ref[0, near:past, :].astype(BF16), cv_ref[0, near:past, :].astype(BF16),
                jnp.concatenate([bnear, bnear], axis=0), m_scr, l_scr, acc_scr)
    bnew = bnew_ref[0, 0]
    _flash_step(qq, kn_ref[...], vn_ref[...], jnp.concatenate([bnew, bnew], axis=0), m_scr, l_scr, acc_scr)
    od_ref[...] = _flash_finish(t, _lam(lamv_ref, lam_init), lam_init, gda_ref, l_scr, acc_scr).astype(BF16)

    o, s_new = _retention_step(qr_ref[...], kr_ref[...], vr_ref[...], s0_ref[0, 0],
                               dmat_ref[0], dq_ref[0], dk_ref[0], dc_ref[0, 0:1])
    sout_ref[0, 0] = s_new
    or_ref[...] = (_rms(o, grt_ref[...]) * sg_ref[...].astype(F32)).astype(BF16)


def _mixer_sample(lam_init, b, t, past, q, ck, cv, kn, vn, bnear, bnew, rel_bias, qr, kr, vr, sg, s0, lamv, gda, grt):
    n = b * t
    far = past - NEAR_WINDOW
    kc = next(c for c in (768, 512, 256) if far % c == 0)
    dmat, dq, dk, dc = _decay_tables(t)
    tile = pl.BlockSpec((t, HEAD_W), lambda bi, h: (bi, h))
    cache = pl.BlockSpec((1, past, HEAD_W), lambda bi, h: (bi, 0, h))
    per_head = lambda *s: pl.BlockSpec((1,) + s, lambda bi, h: (h,) + (0,) * len(s))
    state = pl.BlockSpec((1, 1, RT_KD, HEAD_W), lambda bi, h: (bi, h, 0, 0))
    vec = lambda w: pl.BlockSpec((1, w), lambda bi, h: (0, 0))
    return pl.pallas_call(
        functools.partial(_mixer_sample_kernel, lam_init, kc),
        grid=(b, HEADS),
        in_specs=[tile, cache, cache, tile, tile, per_head(1, t, NEAR_WINDOW), per_head(1, t, t),
                  pl.BlockSpec(memory_space=pltpu.SMEM), tile, tile, tile, tile,
                  per_head(t, t), per_head(t, HEAD_W), per_head(t, HEAD_W), per_head(8, HEAD_W),
                  state, pl.BlockSpec((4, DA_HD), lambda bi, h: (0, 0)), vec(HEAD_W), vec(HEAD_W)],
        out_specs=[tile, tile, state],
        out_shape=[jax.ShapeDtypeStruct((n, SEG), BF16), jax.ShapeDtypeStruct((n, SEG), BF16),
                   jax.ShapeDtypeStruct((b, HEADS, RT_KD, HEAD_W), F32)],
        scratch_shapes=[pltpu.VMEM((2 * t, HEAD_W), F32), pltpu.VMEM((2 * t, HEAD_W), F32),
                        pltpu.VMEM((2 * t, HEAD_W), F32)],
        compiler_params=_cparams("arbitrary", "arbitrary"),
        name="mixer_sample",
    )(q, ck, cv, kn, vn, bnear, bnew, rel_bias, qr, kr, vr, sg, dmat, dq, dk, dc, s0, lamv, gda, grt)


def _channel_kernel(nseq, rows, tiles_per_seq, x_ref, od_ref, or_ref, sd_ref, sr_ref, p_ref, st_ref,
                    wbd_ref, wbr_ref, wo_ref, gffn_ref, wg_ref, wu_ref, cw_ref, cb_ref, wd_ref,
                    gpe_ref, wpe_ref, wpg_ref, y_ref, cn_ref, gbuf):
    i = pl.program_id(0)
    dot = functools.partial(jnp.dot, preferred_element_type=F32)
    mix = (sd_ref[...].astype(F32) * dot(od_ref[...], wbd_ref[...])
           + sr_ref[...].astype(F32) * dot(or_ref[...], wbr_ref[...]))
    h1 = x_ref[...] + dot(mix.astype(BF16), wo_ref[...])

    f = _rms(h1, gffn_ref[...]).astype(BF16)
    g = dot(f, wg_ref[...])
    u = dot(f, wu_ref[...])
    cw = cw_ref[...]
    stride = rows + 8
    pieces = []
    for j in range(nseq):
        base = j * stride
        if tiles_per_seq == 1:
            gbuf[base + 6:base + 8, :] = st_ref[j]
        else:
            @pl.when(i % tiles_per_seq == 0)
            def _():
                gbuf[6:8, :] = st_ref[0]
        gj = g[j * rows:(j + 1) * rows]
        gbuf[base + 8:base + 8 + rows, :] = gj
        gc = (cb_ref[...] + gbuf[base + 6:base + 6 + rows, :] * cw[0:1]
              + gbuf[base + 7:base + 7 + rows, :] * cw[1:2] + gj * cw[2:3])
        pieces.append(gc)
        last2 = gbuf[base + rows + 6:base + rows + 8, :]
        cn_ref[j] = last2
        if tiles_per_seq > 1:
            gbuf[6:8, :] = last2
    gc = pieces[0] if nseq == 1 else jnp.concatenate(pieces, axis=0)
    act = (jax.nn.gelu(gc) * u).astype(BF16)
    h2 = h1 + dot(act, wd_ref[...])

    gate = jax.nn.sigmoid(dot(_rms(h2, gpe_ref[...]).astype(BF16), wpg_ref[...]))
    y_ref[...] = h2 + dot(p_ref[...].astype(BF16), wpe_ref[...]) * gate


def _channel(b, t, tm, x2, od, orr, sd, sr, p2, st, w):
    n = b * t
    rows = min(tm, t)
    nseq = tm // rows
    tiles_per_seq = t // rows
    row = lambda wd: pl.BlockSpec((tm, wd), lambda i: (i, 0))
    if tiles_per_seq == 1:
        st_spec = pl.BlockSpec((nseq, CONV_W - 1, D_FF), lambda i: (i, 0, 0))
    else:
        st_spec = pl.BlockSpec((1, CONV_W - 1, D_FF), lambda i: (i // tiles_per_seq, 0, 0))
    weights = [w["w_bd"], w["w_br"], w["w_o"], w["g_ffn"], w["w_g"], w["w_u"], w["conv_w"], w["conv_b"],
               w["w_d"], w["g_pe"], w["w_pe"], w["w_pg"]]
    return pl.pallas_call(
        functools.partial(_channel_kernel, nseq, rows, tiles_per_seq),
        grid=(n // tm,),
        in_specs=[row(D_MODEL), row(SEG), row(SEG), row(D_MODEL), row(D_MODEL), row(PE_DIM), st_spec]
                 + [_const_spec(a.shape) for a in weights],
        out_specs=[row(D_MODEL), st_spec],
        out_shape=[jax.ShapeDtypeStruct((n, D_MODEL), F32), jax.ShapeDtypeStruct((b, CONV_W - 1, D_FF), F32)],
        scratch_shapes=[pltpu.VMEM((nseq * (rows + 8), D_FF), F32)],
        compiler_params=_cparams("arbitrary"),
        name="channel",
    )(x2, od, orr, sd, sr, p2, st, *weights)


def _rope_tables(pos):
    half = HEAD_W // 2
    inv = jnp.power(ROPE_BASE, -jnp.arange(half, dtype=F32) / half)
    ang = pos.astype(F32)[:, None] * inv[None, :]
    cos, sin = jnp.cos(ang), jnp.sin(ang)
    return jnp.concatenate([cos, cos], axis=1), jnp.concatenate([-sin, sin], axis=1)


def _prep_weights(i, g_mix, w_in, g_q, g_k, lam_q1, lam_k1, lam_q2, lam_k2, g_da, g_rt, w_bd, w_br, w_o,
                  g_ffn, w_g, w_u, conv_w, conv_b, w_d, g_pe, w_pe, w_pg):
    grp = np.arange(256) // DA_HD
    return dict(
        g_mix=g_mix[i][None], w_in=w_in[i].astype(BF16),
        g_q=jnp.tile(g_q[i], SEG // DA_HD)[None], g_k=jnp.tile(g_k[i], SEG // DA_HD)[None],
        gmat=jnp.asarray((grp[:, None] == grp[None, :]) / DA_HD, BF16),
        lamv=jnp.stack([lam_q1[i], lam_k1[i], lam_q2[i], lam_k2[i]]),
        g_da=g_da[i][None], g_rt=g_rt[i][None],
        w_bd=w_bd[i].astype(BF16), w_br=w_br[i].astype(BF16), w_o=w_o[i].astype(BF16),
        g_ffn=g_ffn[i][None], w_g=w_g[i].astype(BF16), w_u=w_u[i].astype(BF16),
        conv_w=conv_w[i], conv_b=conv_b[i][None], w_d=w_d[i].astype(BF16),
        g_pe=g_pe[i][None], w_pe=w_pe[i].astype(BF16), w_pg=w_pg[i].astype(BF16))


def _layer(h, pe, cache_k, cache_v, s_ret, s_conv, rel_bias, lam_init, w):
    b, t, _ = h.shape
    n = b * t
    past = 0 if cache_k is None else cache_k.shape[1]
    x2 = h.reshape(n, D_MODEL)
    qpos = past + jnp.arange(t, dtype=jnp.int32)

    tm_in = min(512, n)
    cos_t, sin_t = _rope_tables(qpos)
    if t < tm_in:
        cos_t, sin_t = jnp.tile(cos_t, (tm_in // t, 1)), jnp.tile(sin_t, (tm_in // t, 1))
    q, k32, kb, v32, vb, qr, kr, vr, sg, sd, sr = _inproj(
        x2, cos_t, sin_t, w["g_mix"], w["w_in"], w["g_q"], w["g_k"], w["gmat"], tm_in)

    if cache_k is None:
        tq = ATTN_BLOCK
        assert t % tq == 0 and tq % CHUNK == 0
        off = jnp.arange(tq, dtype=jnp.int32)
        qp = jnp.stack([off + 2 * tq] * 3)
        kp = jnp.stack([off + 2 * tq, off + tq, off])
        bias = _bias_tiles(rel_bias, qp, kp)
        s0 = jnp.zeros((b, HEADS, RT_KD, HEAD_W), F32)
        od, orr, s_new = _mixer_prompt(lam_init, b, t, q, kb, vb, bias, qr, kr, vr, sg, s0,
                                       w["lamv"], w["g_da"], w["g_rt"])
        st = jnp.zeros((b, CONV_W - 1, D_FF), F32)
    else:
        assert past % CHUNK == 0 and t <= CHUNK and past >= NEAR_WINDOW + MAX_DIST
        near = jnp.arange(past - NEAR_WINDOW, past, dtype=jnp.int32)
        bnear = _bias_tiles(rel_bias, qpos[None], near[None])
        bnew = _bias_tiles(rel_bias, qpos[None], qpos[None])
        od, orr, s_new = _mixer_sample(
            lam_init, b, t, past, q, cache_k.reshape(b, past, SEG), cache_v.reshape(b, past, SEG), kb, vb,
            bnear, bnew, rel_bias, qr, kr, vr, sg, s_ret, w["lamv"], w["g_da"], w["g_rt"])
        st = s_conv

    tm_ch = min(256, n)
    y, conv_new = _channel(b, t, tm_ch, x2, od, orr, sd, sr, pe.reshape(n, PE_DIM), st, w)
    return (y.reshape(b, t, D_MODEL), k32.reshape(b, t, HEADS, HEAD_W), v32.reshape(b, t, HEADS, HEAD_W),
            s_new, conv_new)


def kernel(x_prompt, x_sample, p_prompt, p_sample, cache_k, cache_v, state_ret, state_conv, rel_bias, g_mix, w_in, g_q, g_k, lam_q1, lam_k1, lam_q2, lam_k2, g_da, g_rt, w_bd, w_br, w_o, g_ffn, w_g, w_u, conv_w, conv_b, w_d, g_pe, w_pe, w_pg):
    depth = w_in.shape[0]
    hp, hs = x_prompt, x_sample
    outs = [[] for _ in range(8)]
    for i in range(depth):
        lam_init = 0.8 - 0.6 * math.exp(-0.3 * i)
        w = _prep_weights(i, g_mix, w_in, g_q, g_k, lam_q1, lam_k1, lam_q2, lam_k2, g_da, g_rt, w_bd, w_br,
                          w_o, g_ffn, w_g, w_u, conv_w, conv_b, w_d, g_pe, w_pe, w_pg)
        hp, kp, vp, rp, cp = _layer(hp, p_prompt[i], None, None, None, None, rel_bias, lam_init, w)
        hs, ks, vs, rs, cs = _layer(hs, p_sample[i], cache_k[i], cache_v[i], state_ret[i], state_conv[i],
                                    rel_bias, lam_init, w)
        for lst, val in zip(outs, (kp, vp, rp, cp, ks, vs, rs, cs)):
            lst.append(val)
    return (hp, hs) + tuple(jnp.stack(o) for o in outs)
```

```python
import functools
import math

import numpy as np
import jax
import jax.numpy as jnp
from jax import lax
from jax.experimental import pallas as pl
from jax.experimental.pallas import tpu as pltpu

D_MODEL = 1024
CHUNK = 64
PE_DIM = 256
HEADS = 4
HEAD_W = 128
DA_HD = 64
RT_KD = 128
D_FF = 2816
CONV_W = 3
N_BUCKETS = 32
MAX_DIST = 128
EPS = 1e-6
ROPE_BASE = 10000.0
SEG = HEADS * HEAD_W
D_IN = 7 * SEG + 2 * D_MODEL

MASK_VALUE = -1e30
V7X_VMEM_LIMIT = 56 * 1024 * 1024
ATTN_BLOCK = 256
F32 = jnp.float32
BF16 = jnp.bfloat16


def _cparams(*sem):
    return pltpu.CompilerParams(dimension_semantics=sem, vmem_limit_bytes=V7X_VMEM_LIMIT)


def _const_spec(shape):
    nd = len(shape)
    return pl.BlockSpec(shape, lambda *_: (0,) * nd, pipeline_mode=pl.Buffered(1))


def _rms(x, g):
    return x * lax.rsqrt(jnp.mean(x * x, axis=-1, keepdims=True) + EPS) * g


def _t5_bucket(rel):
    nb = N_BUCKETS // 2
    ret = jnp.where(rel > 0, nb, 0)
    n = jnp.abs(rel)
    max_exact = nb // 2
    nf = jnp.maximum(n, 1).astype(F32)
    large = max_exact + (jnp.log(nf / max_exact) / math.log(MAX_DIST / max_exact) * (nb - max_exact)).astype(jnp.int32)
    large = jnp.minimum(large, nb - 1)
    return ret + jnp.where(n < max_exact, n, large)


FAR_BUCKET = N_BUCKETS // 2 - 1
LOG2E = math.log2(math.e)


def _bias_kernel(idx_ref, madd_ref, rb_ref, out_ref):
    h = pl.program_id(0)
    idx = idx_ref[...]
    bias = jnp.zeros(idx.shape, F32)
    for b in range(N_BUCKETS):
        bias = jnp.where(idx == b, rb_ref[b, h], bias)
    out_ref[0] = (bias - rb_ref[FAR_BUCKET, h]) * LOG2E + madd_ref[...]


def _bias_tiles(rel_bias, qpos, kpos):
    rel = kpos[:, None, :] - qpos[:, :, None]
    idx = _t5_bucket(rel).astype(jnp.int32)
    madd = jnp.where((kpos[:, None, :] // CHUNK) <= (qpos[:, :, None] // CHUNK), 0.0, MASK_VALUE).astype(F32)
    n, r, c = idx.shape
    return pl.pallas_call(
        _bias_kernel,
        grid=(HEADS,),
        in_specs=[pl.BlockSpec((n, r, c), lambda h: (0, 0, 0)),
                  pl.BlockSpec((n, r, c), lambda h: (0, 0, 0)),
                  pl.BlockSpec(memory_space=pltpu.SMEM)],
        out_specs=pl.BlockSpec((1, n, r, c), lambda h: (h, 0, 0, 0)),
        out_shape=jax.ShapeDtypeStruct((HEADS, n, r, c), F32),
        compiler_params=_cparams("arbitrary"),
        name="bias_tiles",
    )(idx, madd, rel_bias)


def _inproj_kernel(x_ref, cos_ref, sin_ref, gmix_ref, w_ref, gq_ref, gk_ref, gmat_ref,
                   q_ref, k32_ref, kb_ref, v32_ref, vb_ref, qr_ref, kr_ref, vr_ref, sg_ref, sd_ref, sr_ref):
    a = _rms(x_ref[...], gmix_ref[...]).astype(BF16)

    def seg(c0, c1):
        return jnp.dot(a, w_ref[:, c0:c1], preferred_element_type=F32)

    def group_norm(z, g):
        sq = (z * z).astype(BF16)
        outs = []
        for c in range(0, SEG, 256):
            ms = jnp.dot(sq[:, c:c + 256], gmat_ref[...], preferred_element_type=F32)
            outs.append(z[:, c:c + 256] * lax.rsqrt(ms + EPS))
        return jnp.concatenate(outs, axis=1) * g

    def rotary(z):
        cos, sin = cos_ref[...], sin_ref[...]
        outs = []
        for h in range(HEADS):
            zs = z[:, h * HEAD_W:(h + 1) * HEAD_W]
            outs.append(zs * cos + pltpu.roll(zs, HEAD_W // 2, 1) * sin)
        return jnp.concatenate(outs, axis=1)

    q_ref[...] = (group_norm(seg(0, SEG), gq_ref[...]) * (DA_HD ** -0.5 * LOG2E)).astype(BF16)
    k = group_norm(seg(SEG, 2 * SEG), gk_ref[...])
    kb_ref[...] = k.astype(BF16)
    v = seg(2 * SEG, 3 * SEG)
    vb_ref[...] = v.astype(BF16)
    tm = k.shape[0]
    for h in range(HEADS):
        k32_ref[pl.ds(h, tm, stride=HEADS), :] = k[:, h * HEAD_W:(h + 1) * HEAD_W]
        v32_ref[pl.ds(h, tm, stride=HEADS), :] = v[:, h * HEAD_W:(h + 1) * HEAD_W]
    qr_ref[...] = rotary(seg(3 * SEG, 4 * SEG)).astype(BF16)
    kr_ref[...] = (rotary(seg(4 * SEG, 5 * SEG)) * RT_KD ** -0.5).astype(BF16)
    vr_ref[...] = seg(5 * SEG, 6 * SEG).astype(BF16)
    gr = seg(6 * SEG, 7 * SEG)
    sg_ref[...] = (gr * jax.nn.sigmoid(gr)).astype(BF16)
    sd_ref[...] = jax.nn.sigmoid(seg(7 * SEG, 7 * SEG + D_MODEL)).astype(BF16)
    sr_ref[...] = jax.nn.sigmoid(seg(7 * SEG + D_MODEL, D_IN)).astype(BF16)


def _inproj(x2, cos_t, sin_t, gmix, w_in, gq, gk, gmat, tm):
    n = x2.shape[0]
    nt = cos_t.shape[0] // tm
    row = lambda w: pl.BlockSpec((tm, w), lambda i: (i, 0))
    tab = pl.BlockSpec((tm, HEAD_W), lambda i: (i % nt, 0))
    out_w = [(SEG, BF16), (None, F32), (SEG, BF16), (None, F32), (SEG, BF16), (SEG, BF16), (SEG, BF16),
             (SEG, BF16), (SEG, BF16), (D_MODEL, BF16), (D_MODEL, BF16)]
    by_head = pl.BlockSpec((tm * HEADS, HEAD_W), lambda i: (i, 0))
    return pl.pallas_call(
        _inproj_kernel,
        grid=(n // tm,),
        in_specs=[row(D_MODEL), tab, tab, _const_spec((1, D_MODEL)), _const_spec((D_MODEL, D_IN)),
                  _const_spec((1, SEG)), _const_spec((1, SEG)), _const_spec((256, 256))],
        out_specs=[by_head if w is None else row(w) for w, _ in out_w],
        out_shape=[jax.ShapeDtypeStruct((n * HEADS, HEAD_W) if w is None else (n, w), dt) for w, dt in out_w],
        compiler_params=_cparams("arbitrary"),
        name="inproj",
    )(x2, cos_t, sin_t, gmix, w_in, gq, gk, gmat)


def _split_maps(q):
    lane = lax.broadcasted_iota(jnp.int32, q.shape, 1)
    zero = jnp.zeros_like(q)
    return jnp.concatenate([jnp.where(lane < DA_HD, q, zero), jnp.where(lane >= DA_HD, q, zero)], axis=0)


def _flash_init(m_scr, l_scr, acc_scr):
    m_scr[...] = jnp.full(m_scr.shape, MASK_VALUE, F32)
    l_scr[...] = jnp.zeros(l_scr.shape, F32)
    acc_scr[...] = jnp.zeros(acc_scr.shape, F32)


def _flash_step(qq, k, v, bias, m_scr, l_scr, acc_scr):
    s = lax.dot_general(qq, k, (((1,), (1,)), ((), ())), preferred_element_type=F32)
    if bias is not None:
        s = s + bias
    m_prev = m_scr[...]
    m_new = jnp.maximum(m_prev, jnp.max(s, axis=1, keepdims=True))
    alpha = jnp.exp2(m_prev - m_new)
    p = jnp.exp2(s - m_new[:, :1])
    l_scr[...] = alpha * l_scr[...] + jnp.sum(p, axis=1, keepdims=True)
    acc_scr[...] = alpha * acc_scr[...] + jnp.dot(p.astype(BF16), v, preferred_element_type=F32)
    m_scr[...] = m_new


def _flash_step_heads(qqs, ks, vs, biases, m_scr, l_scr, acc_scr):
    nt = (((1,), (1,)), ((), ()))
    ss = [lax.dot_general(qq, k, nt, preferred_element_type=F32) for qq, k in zip(qqs, ks)]
    ss = [s if b is None else s + b for s, b in zip(ss, biases)]
    m_prev = [m_scr[h] for h in range(HEADS)]
    l_prev = [l_scr[h] for h in range(HEADS)]
    acc_prev = [acc_scr[h] for h in range(HEADS)]
    m_new = [jnp.maximum(mp, jnp.max(s, axis=1, keepdims=True)) for mp, s in zip(m_prev, ss)]
    alpha = [jnp.exp2(mp - mn) for mp, mn in zip(m_prev, m_new)]
    ps = [jnp.exp2(s - mn[:, :1]) for s, mn in zip(ss, m_new)]
    pv = [jnp.dot(p.astype(BF16), v, preferred_element_type=F32) for p, v in zip(ps, vs)]
    for h in range(HEADS):
        l_scr[h] = alpha[h] * l_prev[h] + jnp.sum(ps[h], axis=1, keepdims=True)
        acc_scr[h] = alpha[h] * acc_prev[h] + pv[h]
        m_scr[h] = m_new[h]


FLASH_STRIP = 64


def _flash_step_staged(qqs, ks, vs, bias_fn, m_scr, l_scr, acc_scr, s_scr, p_scr, a_scr):
    _, rows, tk = s_scr.shape
    for h in range(HEADS):
        s_scr[h] = lax.dot_general(qqs[h], ks[h], (((1,), (1,)), ((), ())), preferred_element_type=F32)
        for r0 in range(0, rows, FLASH_STRIP):
            rs = slice(r0, r0 + FLASH_STRIP)
            s = s_scr[h, rs, :]
            if bias_fn is not None:
                s = s + bias_fn(h, r0, FLASH_STRIP)
            m_prev = m_scr[h, rs, :]
            m_new = jnp.maximum(m_prev, jnp.max(s, axis=1, keepdims=True))
            a_scr[h, rs, :] = jnp.exp2(m_prev - m_new)
            m_scr[h, rs, :] = m_new
            p_scr[h, rs, :] = jnp.exp2(s - jnp.concatenate([m_new] * (tk // HEAD_W), axis=1)).astype(BF16)
        v = vs[h]
        pv = jnp.dot(p_scr[h], jnp.concatenate([v, jnp.ones_like(v)], axis=1), preferred_element_type=F32)
        alpha = a_scr[h]
        acc_scr[h] = alpha * acc_scr[h] + pv[:, :HEAD_W]
        l_scr[h] = alpha * l_scr[h] + pv[:, HEAD_W:]


def _lam(lamv_ref, lam_init):
    lv = lamv_ref[...]
    s1 = jnp.sum(lv[0:1] * lv[1:2], axis=1, keepdims=True)
    s2 = jnp.sum(lv[2:3] * lv[3:4], axis=1, keepdims=True)
    return jnp.exp(s1) - jnp.exp(s2) + lam_init


def _flash_finish(t, lam, lam_init, gda_ref, l_scr, acc_scr):
    o = acc_scr[...] / l_scr[...]
    o = o[:t] - lam * o[t:]
    return _rms(o, gda_ref[...]) * (1.0 - lam_init)


def _retention_step(qr, kr, vr, s_prev, dmat, dq, dk, dc):
    att = lax.dot_general(qr, kr, (((1,), (1,)), ((), ())), preferred_element_type=F32) * dmat
    o = jnp.dot(att.astype(BF16), vr, preferred_element_type=F32)
    o = o + jnp.dot(qr, s_prev.astype(BF16), preferred_element_type=F32) * dq
    kdec = (kr.astype(F32) * dk).astype(BF16)
    s_new = s_prev * dc + lax.dot_general(kdec, vr, (((0,), (0,)), ((), ())), preferred_element_type=F32)
    return o, s_new


def _decay_tables(c):
    log_g = jnp.log1p(-jnp.power(2.0, -5.0 - jnp.arange(HEADS, dtype=F32)))
    idx = jnp.arange(c, dtype=F32)
    diff = idx[:, None] - idx[None, :]
    dmat = jnp.where(diff >= 0, jnp.exp(log_g[:, None, None] * jnp.maximum(diff, 0.0)), 0.0)
    dq = jnp.broadcast_to(jnp.exp(log_g[:, None] * (idx + 1.0)[None, :])[:, :, None], (HEADS, c, HEAD_W))
    dk = jnp.broadcast_to(jnp.exp(log_g[:, None] * (c - 1.0 - idx)[None, :])[:, :, None], (HEADS, c, HEAD_W))
    dc = jnp.broadcast_to(jnp.exp(log_g * c)[:, None, None], (HEADS, 8, HEAD_W))
    return dmat.astype(F32), dq.astype(F32), dk.astype(F32), dc.astype(F32)


def _head(h):
    return slice(h * HEAD_W, (h + 1) * HEAD_W)


def _mixer_prompt_kernel(lam_init, q_ref, k_ref, v_ref, bias_ref, qr_ref, kr_ref, vr_ref, sg_ref,
                         dmat_ref, dq_ref, dk_ref, dc_ref, s0_ref, lamv_ref, gda_ref, grt_ref,
                         od_ref, or_ref, sout_ref, m_scr, l_scr, acc_scr, sc_scr, p_scr, a_scr, s_scr):
    qb = pl.program_id(1)
    t = q_ref.shape[0]
    qq = [_split_maps(q_ref[:, _head(h)]) for h in range(HEADS)]
    _flash_init(m_scr, l_scr, acc_scr)

    def key_block(kb, d):
        off = pl.multiple_of(kb * t, t)
        bias_fn = None if d is None else (lambda h, r0, n: bias_ref[h, d, r0 % t:r0 % t + n, :])
        _flash_step_staged(qq, [k_ref[pl.ds(off, t), _head(h)] for h in range(HEADS)],
                           [v_ref[pl.ds(off, t), _head(h)] for h in range(HEADS)], bias_fn,
                           m_scr, l_scr, acc_scr, sc_scr, p_scr, a_scr)

    def far_body(kb, carry):
        key_block(kb, None)
        return carry

    lax.fori_loop(0, qb - 1, far_body, 0)

    @pl.when(qb >= 1)
    def _():
        key_block(qb - 1, 1)

    key_block(qb, 0)
    lam = _lam(lamv_ref, lam_init)

    @pl.when(qb == 0)
    def _():
        s_scr[...] = s0_ref[0]

    for h in range(HEADS):
        od_ref[:, _head(h)] = _flash_finish(t, lam, lam_init, gda_ref, l_scr.at[h], acc_scr.at[h]).astype(BF16)
        o, s_new = _retention_step(qr_ref[:, _head(h)], kr_ref[:, _head(h)], vr_ref[:, _head(h)], s_scr[h],
                                   dmat_ref[h], dq_ref[h], dk_ref[h], dc_ref[h, 0:1])
        s_scr[h] = s_new
        sout_ref[0, h] = s_new
        or_ref[:, _head(h)] = (_rms(o, grt_ref[...]) * sg_ref[:, _head(h)].astype(F32)).astype(BF16)


def _mixer_prompt(lam_init, b, t, q, kb, vb, bias, qr, kr, vr, sg, s0, lamv, gda, grt):
    tq = ATTN_BLOCK
    nq = t // tq
    n = b * t
    tables = _decay_tables(tq)
    tile = pl.BlockSpec((tq, SEG), lambda bi, qb: (bi * nq + qb, 0))
    seq = pl.BlockSpec((t, SEG), lambda bi, qb: (bi, 0))
    state = pl.BlockSpec((1, HEADS, RT_KD, HEAD_W), lambda bi, qb: (bi, 0, 0, 0))
    return pl.pallas_call(
        functools.partial(_mixer_prompt_kernel, lam_init),
        grid=(b, nq),
        in_specs=[tile, seq, seq, _const_spec(bias.shape), tile, tile, tile, tile]
                 + [_const_spec(a.shape) for a in tables]
                 + [state, _const_spec(lamv.shape), _const_spec(gda.shape), _const_spec(grt.shape)],
        out_specs=[tile, tile, state],
        out_shape=[jax.ShapeDtypeStruct((n, SEG), BF16), jax.ShapeDtypeStruct((n, SEG), BF16),
                   jax.ShapeDtypeStruct((b, HEADS, RT_KD, HEAD_W), F32)],
        scratch_shapes=[pltpu.VMEM((HEADS, 2 * tq, HEAD_W), F32), pltpu.VMEM((HEADS, 2 * tq, HEAD_W), F32),
                        pltpu.VMEM((HEADS, 2 * tq, HEAD_W), F32), pltpu.VMEM((HEADS, 2 * tq, tq), F32),
                        pltpu.VMEM((HEADS, 2 * tq, tq), BF16), pltpu.VMEM((HEADS, 2 * tq, HEAD_W), F32),
                        pltpu.VMEM((HEADS, RT_KD, HEAD_W), F32)],
        compiler_params=_cparams("arbitrary", "arbitrary"),
        name="mixer_prompt",
    )(q, kb, vb, bias, qr, kr, vr, sg, *tables, s0, lamv, gda, grt)


CACHE_CHUNK = 1024


def _mixer_sample_kernel(lam_init, q_ref, ck_ref, cv_ref, kn_ref, vn_ref, bnear_ref, bnew_ref,
                         qr_ref, kr_ref, vr_ref, sg_ref, dmat_ref, dq_ref, dk_ref, dc_ref, s0_ref,
                         lamv_ref, gda_ref, grt_ref, od_ref, or_ref, sout_ref, m_scr, l_scr, acc_scr):
    c = pl.program_id(1)
    last = pl.num_programs(1) - 1
    t = q_ref.shape[0]
    qq = [_split_maps(q_ref[:, _head(h)]) for h in range(HEADS)]

    @pl.when(c == 0)
    def _():
        _flash_init(m_scr, l_scr, acc_scr)

    def cache_keys(bias_ref):
        rows = [pl.ds(h, CACHE_CHUNK, stride=HEADS) for h in range(HEADS)]
        _flash_step_heads(
            qq, [ck_ref[r, :].astype(BF16) for r in rows], [cv_ref[r, :].astype(BF16) for r in rows],
            [None if bias_ref is None else jnp.concatenate([bias_ref[h, 0]] * 2, axis=0) for h in range(HEADS)],
            m_scr, l_scr, acc_scr)

    @pl.when(c < last)
    def _():
        cache_keys(None)

    @pl.when(c == last)
    def _():
        cache_keys(bnear_ref)
        lam = _lam(lamv_ref, lam_init)
        _flash_step_heads(
            qq, [kn_ref[:, _head(h)] for h in range(HEADS)], [vn_ref[:, _head(h)] for h in range(HEADS)],
            [jnp.concatenate([bnew_ref[h, 0]] * 2, axis=0) for h in range(HEADS)], m_scr, l_scr, acc_scr)
        for h in range(HEADS):
            od_ref[:, _head(h)] = _flash_finish(t, lam, lam_init, gda_ref, l_scr.at[h], acc_scr.at[h]).astype(BF16)
            o, s_new = _retention_step(qr_ref[:, _head(h)], kr_ref[:, _head(h)], vr_ref[:, _head(h)], s0_ref[0, h],
                                       dmat_ref[h], dq_ref[h], dk_ref[h], dc_ref[h, 0:1])
            sout_ref[0, h] = s_new
            or_ref[:, _head(h)] = (_rms(o, grt_ref[...]) * sg_ref[:, _head(h)].astype(F32)).astype(BF16)


def _mixer_sample(lam_init, layer, b, t, past, q, cache_k, cache_v, kn, vn, bnear, bnew, qr, kr, vr, sg, s0,
                  lamv, gda, grt):
    n = b * t
    assert past % CACHE_CHUNK == 0
    tables = _decay_tables(t)
    tile = pl.BlockSpec((t, SEG), lambda bi, c: (bi, 0))
    chunks = past // CACHE_CHUNK
    cache_k, cache_v = cache_k.reshape(-1, HEAD_W), cache_v.reshape(-1, HEAD_W)
    cache = pl.BlockSpec((CACHE_CHUNK * HEADS, HEAD_W), lambda bi, c: ((layer * b + bi) * chunks + c, 0))
    state = pl.BlockSpec((1, HEADS, RT_KD, HEAD_W), lambda bi, c: (bi, 0, 0, 0))
    consts = [bnear, bnew]
    return pl.pallas_call(
        functools.partial(_mixer_sample_kernel, lam_init),
        grid=(b, past // CACHE_CHUNK),
        in_specs=[tile, cache, cache, tile, tile] + [_const_spec(a.shape) for a in consts]
                 + [tile, tile, tile, tile] + [_const_spec(a.shape) for a in tables]
                 + [state, _const_spec(lamv.shape), _const_spec(gda.shape), _const_spec(grt.shape)],
        out_specs=[tile, tile, state],
        out_shape=[jax.ShapeDtypeStruct((n, SEG), BF16), jax.ShapeDtypeStruct((n, SEG), BF16),
                   jax.ShapeDtypeStruct((b, HEADS, RT_KD, HEAD_W), F32)],
        scratch_shapes=[pltpu.VMEM((HEADS, 2 * t, HEAD_W), F32), pltpu.VMEM((HEADS, 2 * t, HEAD_W), F32),
                        pltpu.VMEM((HEADS, 2 * t, HEAD_W), F32)],
        compiler_params=_cparams("arbitrary", "arbitrary"),
        name="mixer_sample",
    )(q, cache_k, cache_v, kn, vn, bnear, bnew, qr, kr, vr, sg, *tables, s0, lamv, gda, grt)


def _channel_kernel(nseq, rows, tiles_per_seq, x_ref, od_ref, or_ref, sd_ref, sr_ref, p_ref, st_ref,
                    wbd_ref, wbr_ref, wo_ref, gffn_ref, wg_ref, wu_ref, cw_ref, cb_ref, wd_ref,
                    gpe_ref, wpe_ref, wpg_ref, y_ref, cn_ref, gbuf):
    i = pl.program_id(0)
    dot = functools.partial(jnp.dot, preferred_element_type=F32)
    mix = (sd_ref[...].astype(F32) * dot(od_ref[...], wbd_ref[...])
           + sr_ref[...].astype(F32) * dot(or_ref[...], wbr_ref[...]))
    h1 = x_ref[...] + dot(mix.astype(BF16), wo_ref[...])

    f = _rms(h1, gffn_ref[...]).astype(BF16)
    g = dot(f, wg_ref[...])
    u = dot(f, wu_ref[...])
    cw = cw_ref[...]
    stride = rows + 8
    pieces = []
    for j in range(nseq):
        base = j * stride
        if tiles_per_seq == 1:
            gbuf[base + 6:base + 8, :] = st_ref[j]
        else:
            @pl.when(i % tiles_per_seq == 0)
            def _():
                gbuf[6:8, :] = st_ref[0]
        gj = g[j * rows:(j + 1) * rows]
        gbuf[base + 8:base + 8 + rows, :] = gj
        gc = (cb_ref[...] + gbuf[base + 6:base + 6 + rows, :] * cw[0:1]
              + gbuf[base + 7:base + 7 + rows, :] * cw[1:2] + gj * cw[2:3])
        pieces.append(gc)
        last2 = gbuf[base + rows + 6:base + rows + 8, :]
        cn_ref[j] = last2
        if tiles_per_seq > 1:
            gbuf[6:8, :] = last2
    gc = pieces[0] if nseq == 1 else jnp.concatenate(pieces, axis=0)
    act = (jax.nn.gelu(gc) * u).astype(BF16)
    h2 = h1 + dot(act, wd_ref[...])

    gate = jax.nn.sigmoid(dot(_rms(h2, gpe_ref[...]).astype(BF16), wpg_ref[...]))
    y_ref[...] = h2 + dot(p_ref[...].astype(BF16), wpe_ref[...]) * gate


def _channel(b, t, tm, x2, od, orr, sd, sr, p2, st, w):
    n = b * t
    rows = min(tm, t)
    nseq = tm // rows
    tiles_per_seq = t // rows
    row = lambda wd: pl.BlockSpec((tm, wd), lambda i: (i, 0))
    if tiles_per_seq == 1:
        st_spec = pl.BlockSpec((nseq, CONV_W - 1, D_FF), lambda i: (i, 0, 0))
    else:
        st_spec = pl.BlockSpec((1, CONV_W - 1, D_FF), lambda i: (i // tiles_per_seq, 0, 0))
    weights = [w["w_bd"], w["w_br"], w["w_o"], w["g_ffn"], w["w_g"], w["w_u"], w["conv_w"], w["conv_b"],
               w["w_d"], w["g_pe"], w["w_pe"], w["w_pg"]]
    return pl.pallas_call(
        functools.partial(_channel_kernel, nseq, rows, tiles_per_seq),
        grid=(n // tm,),
        in_specs=[row(D_MODEL), row(SEG), row(SEG), row(D_MODEL), row(D_MODEL), row(PE_DIM), st_spec]
                 + [_const_spec(a.shape) for a in weights],
        out_specs=[row(D_MODEL), st_spec],
        out_shape=[jax.ShapeDtypeStruct((n, D_MODEL), F32), jax.ShapeDtypeStruct((b, CONV_W - 1, D_FF), F32)],
        scratch_shapes=[pltpu.VMEM((nseq * (rows + 8), D_FF), F32)],
        compiler_params=_cparams("arbitrary"),
        name="channel",
    )(x2, od, orr, sd, sr, p2, st, *weights)


def _rope_tables(pos):
    half = HEAD_W // 2
    inv = jnp.power(ROPE_BASE, -jnp.arange(half, dtype=F32) / half)
    ang = pos.astype(F32)[:, None] * inv[None, :]
    cos, sin = jnp.cos(ang), jnp.sin(ang)
    return jnp.concatenate([cos, cos], axis=1), jnp.concatenate([-sin, sin], axis=1)


def _prep_weights(i, g_mix, w_in, g_q, g_k, lam_q1, lam_k1, lam_q2, lam_k2, g_da, g_rt, w_bd, w_br, w_o,
                  g_ffn, w_g, w_u, conv_w, conv_b, w_d, g_pe, w_pe, w_pg):
    grp = np.arange(256) // DA_HD
    return dict(
        g_mix=g_mix[i][None], w_in=w_in[i].astype(BF16),
        g_q=jnp.tile(g_q[i], SEG // DA_HD)[None], g_k=jnp.tile(g_k[i], SEG // DA_HD)[None],
        gmat=jnp.asarray((grp[:, None] == grp[None, :]) / DA_HD, BF16),
        lamv=jnp.stack([lam_q1[i], lam_k1[i], lam_q2[i], lam_k2[i]]),
        g_da=g_da[i][None], g_rt=g_rt[i][None],
        w_bd=w_bd[i].astype(BF16), w_br=w_br[i].astype(BF16), w_o=w_o[i].astype(BF16),
        g_ffn=g_ffn[i][None], w_g=w_g[i].astype(BF16), w_u=w_u[i].astype(BF16),
        conv_w=conv_w[i], conv_b=conv_b[i][None], w_d=w_d[i].astype(BF16),
        g_pe=g_pe[i][None], w_pe=w_pe[i].astype(BF16), w_pg=w_pg[i].astype(BF16))


def _layer(h, pe, layer, cache_k, cache_v, s_ret, s_conv, rel_bias, lam_init, w):
    b, t, _ = h.shape
    n = b * t
    past = 0 if cache_k is None else cache_k.shape[2]
    x2 = h.reshape(n, D_MODEL)
    qpos = past + jnp.arange(t, dtype=jnp.int32)

    tm_in = min(512, n)
    cos_t, sin_t = _rope_tables(qpos)
    if t < tm_in:
        cos_t, sin_t = jnp.tile(cos_t, (tm_in // t, 1)), jnp.tile(sin_t, (tm_in // t, 1))
    q, k32, kb, v32, vb, qr, kr, vr, sg, sd, sr = _inproj(
        x2, cos_t, sin_t, w["g_mix"], w["w_in"], w["g_q"], w["g_k"], w["gmat"], tm_in)

    if cache_k is None:
        tq = ATTN_BLOCK
        assert t % tq == 0 and tq % CHUNK == 0 and tq >= MAX_DIST
        off = jnp.arange(tq, dtype=jnp.int32)
        qp = jnp.stack([off + tq] * 2)
        kp = jnp.stack([off + tq, off])
        bias = _bias_tiles(rel_bias, qp, kp)
        s0 = jnp.zeros((b, HEADS, RT_KD, HEAD_W), F32)
        od, orr, s_new = _mixer_prompt(lam_init, b, t, q, kb, vb, bias, qr, kr, vr, sg, s0,
                                       w["lamv"], w["g_da"], w["g_rt"])
        st = jnp.zeros((b, CONV_W - 1, D_FF), F32)
    else:
        assert past % CHUNK == 0 and t <= CHUNK and CACHE_CHUNK >= MAX_DIST
        near = jnp.arange(past - CACHE_CHUNK, past, dtype=jnp.int32)
        bnear = _bias_tiles(rel_bias, qpos[None], near[None])
        bnew = _bias_tiles(rel_bias, qpos[None], qpos[None])
        od, orr, s_new = _mixer_sample(lam_init, layer, b, t, past, q, cache_k, cache_v, kb, vb, bnear, bnew,
                                       qr, kr, vr, sg, s_ret, w["lamv"], w["g_da"], w["g_rt"])
        st = s_conv

    tm_ch = min(256, n)
    y, conv_new = _channel(b, t, tm_ch, x2, od, orr, sd, sr, pe.reshape(n, PE_DIM), st, w)
    return (y.reshape(b, t, D_MODEL), k32.reshape(b, t, HEADS, HEAD_W), v32.reshape(b, t, HEADS, HEAD_W),
            s_new, conv_new)


def kernel(x_prompt, x_sample, p_prompt, p_sample, cache_k, cache_v, state_ret, state_conv, rel_bias, g_mix, w_in, g_q, g_k, lam_q1, lam_k1, lam_q2, lam_k2, g_da, g_rt, w_bd, w_br, w_o, g_ffn, w_g, w_u, conv_w, conv_b, w_d, g_pe, w_pe, w_pg):
    depth = w_in.shape[0]
    hp, hs = x_prompt, x_sample
    outs = [[] for _ in range(8)]
    for i in range(depth):
        lam_init = 0.8 - 0.6 * math.exp(-0.3 * i)
        w = _prep_weights(i, g_mix, w_in, g_q, g_k, lam_q1, lam_k1, lam_q2, lam_k2, g_da, g_rt, w_bd, w_br,
                          w_o, g_ffn, w_g, w_u, conv_w, conv_b, w_d, g_pe, w_pe, w_pg)
        hp, kp, vp, rp, cp = _layer(hp, p_prompt[i], i, None, None, None, None, rel_bias, lam_init, w)
        hs, ks, vs, rs, cs = _layer(hs, p_sample[i], i, cache_k, cache_v, state_ret[i], state_conv[i],
                                    rel_bias, lam_init, w)
        for lst, val in zip(outs, (kp, vp, rp, cp, ks, vs, rs, cs)):
            lst.append(val)
    return (hp, hs) + tuple(jnp.stack(o) for o in outs)
```

```python
import functools
import math

import numpy as np
import jax
import jax.numpy as jnp
from jax import lax
from jax.experimental import pallas as pl
from jax.experimental.pallas import tpu as pltpu

D_MODEL = 1024
CHUNK = 64
PE_DIM = 256
HEADS = 4
HEAD_W = 128
DA_HD = 64
RT_KD = 128
D_FF = 2816
CONV_W = 3
N_BUCKETS = 32
MAX_DIST = 128
EPS = 1e-6
ROPE_BASE = 10000.0
SEG = HEADS * HEAD_W
D_IN = 7 * SEG + 2 * D_MODEL

MASK_VALUE = -1e30
V7X_VMEM_LIMIT = 56 * 1024 * 1024
ATTN_BLOCK = 256
F32 = jnp.float32
BF16 = jnp.bfloat16


def _cparams(*sem):
    return pltpu.CompilerParams(dimension_semantics=sem, vmem_limit_bytes=V7X_VMEM_LIMIT)


def _const_spec(shape):
    nd = len(shape)
    return pl.BlockSpec(shape, lambda *_: (0,) * nd, pipeline_mode=pl.Buffered(1))


def _rms(x, g):
    return x * lax.rsqrt(jnp.mean(x * x, axis=-1, keepdims=True) + EPS) * g


def _t5_bucket(rel):
    nb = N_BUCKETS // 2
    ret = jnp.where(rel > 0, nb, 0)
    n = jnp.abs(rel)
    max_exact = nb // 2
    nf = jnp.maximum(n, 1).astype(F32)
    large = max_exact + (jnp.log(nf / max_exact) / math.log(MAX_DIST / max_exact) * (nb - max_exact)).astype(jnp.int32)
    large = jnp.minimum(large, nb - 1)
    return ret + jnp.where(n < max_exact, n, large)


FAR_BUCKET = N_BUCKETS // 2 - 1
LOG2E = math.log2(math.e)


def _bias_kernel(idx_ref, madd_ref, rb_ref, out_ref):
    h = pl.program_id(0)
    idx = idx_ref[...]
    bias = jnp.zeros(idx.shape, F32)
    for b in range(N_BUCKETS):
        bias = jnp.where(idx == b, rb_ref[b, h], bias)
    out_ref[0] = (bias - rb_ref[FAR_BUCKET, h]) * LOG2E + madd_ref[...]


def _bias_tiles(rel_bias, qpos, kpos):
    rel = kpos[:, None, :] - qpos[:, :, None]
    idx = _t5_bucket(rel).astype(jnp.int32)
    madd = jnp.where((kpos[:, None, :] // CHUNK) <= (qpos[:, :, None] // CHUNK), 0.0, MASK_VALUE).astype(F32)
    n, r, c = idx.shape
    return pl.pallas_call(
        _bias_kernel,
        grid=(HEADS,),
        in_specs=[pl.BlockSpec((n, r, c), lambda h: (0, 0, 0)),
                  pl.BlockSpec((n, r, c), lambda h: (0, 0, 0)),
                  pl.BlockSpec(memory_space=pltpu.SMEM)],
        out_specs=pl.BlockSpec((1, n, r, c), lambda h: (h, 0, 0, 0)),
        out_shape=jax.ShapeDtypeStruct((HEADS, n, r, c), F32),
        compiler_params=_cparams("arbitrary"),
        name="bias_tiles",
    )(idx, madd, rel_bias)


def _inproj_kernel(x_ref, cos_ref, sin_ref, gmix_ref, w_ref, gq_ref, gk_ref, gmat_ref,
                   q_ref, k32_ref, kb_ref, v32_ref, vb_ref, qr_ref, kr_ref, vr_ref, sg_ref, sd_ref, sr_ref):
    a = _rms(x_ref[...], gmix_ref[...]).astype(BF16)

    def seg(c0, c1):
        return jnp.dot(a, w_ref[:, c0:c1], preferred_element_type=F32)

    def group_norm(z, g):
        sq = (z * z).astype(BF16)
        outs = []
        for c in range(0, SEG, 256):
            ms = jnp.dot(sq[:, c:c + 256], gmat_ref[...], preferred_element_type=F32)
            outs.append(z[:, c:c + 256] * lax.rsqrt(ms + EPS))
        return jnp.concatenate(outs, axis=1) * g

    def rotary(z):
        cos, sin = cos_ref[...], sin_ref[...]
        outs = []
        for h in range(HEADS):
            zs = z[:, h * HEAD_W:(h + 1) * HEAD_W]
            outs.append(zs * cos + pltpu.roll(zs, HEAD_W // 2, 1) * sin)
        return jnp.concatenate(outs, axis=1)

    q_ref[...] = (group_norm(seg(0, SEG), gq_ref[...]) * (DA_HD ** -0.5 * LOG2E)).astype(BF16)
    k = group_norm(seg(SEG, 2 * SEG), gk_ref[...])
    kb_ref[...] = k.astype(BF16)
    v = seg(2 * SEG, 3 * SEG)
    vb_ref[...] = v.astype(BF16)
    tm = k.shape[0]
    for h in range(HEADS):
        k32_ref[pl.ds(h, tm, stride=HEADS), :] = k[:, h * HEAD_W:(h + 1) * HEAD_W]
        v32_ref[pl.ds(h, tm, stride=HEADS), :] = v[:, h * HEAD_W:(h + 1) * HEAD_W]
    qr_ref[...] = rotary(seg(3 * SEG, 4 * SEG)).astype(BF16)
    kr_ref[...] = (rotary(seg(4 * SEG, 5 * SEG)) * RT_KD ** -0.5).astype(BF16)
    vr_ref[...] = seg(5 * SEG, 6 * SEG).astype(BF16)
    gr = seg(6 * SEG, 7 * SEG)
    sg_ref[...] = (gr * jax.nn.sigmoid(gr)).astype(BF16)
    sd_ref[...] = jax.nn.sigmoid(seg(7 * SEG, 7 * SEG + D_MODEL)).astype(BF16)
    sr_ref[...] = jax.nn.sigmoid(seg(7 * SEG + D_MODEL, D_IN)).astype(BF16)


def _inproj(x2, cos_t, sin_t, gmix, w_in, gq, gk, gmat, tm):
    n = x2.shape[0]
    nt = cos_t.shape[0] // tm
    row = lambda w: pl.BlockSpec((tm, w), lambda i: (i, 0))
    tab = pl.BlockSpec((tm, HEAD_W), lambda i: (i % nt, 0))
    out_w = [(SEG, BF16), (None, F32), (SEG, BF16), (None, F32), (SEG, BF16), (SEG, BF16), (SEG, BF16),
             (SEG, BF16), (SEG, BF16), (D_MODEL, BF16), (D_MODEL, BF16)]
    by_head = pl.BlockSpec((tm * HEADS, HEAD_W), lambda i: (i, 0))
    return pl.pallas_call(
        _inproj_kernel,
        grid=(n // tm,),
        in_specs=[row(D_MODEL), tab, tab, _const_spec((1, D_MODEL)), _const_spec((D_MODEL, D_IN)),
                  _const_spec((1, SEG)), _const_spec((1, SEG)), _const_spec((256, 256))],
        out_specs=[by_head if w is None else row(w) for w, _ in out_w],
        out_shape=[jax.ShapeDtypeStruct((n * HEADS, HEAD_W) if w is None else (n, w), dt) for w, dt in out_w],
        compiler_params=_cparams("arbitrary"),
        name="inproj",
    )(x2, cos_t, sin_t, gmix, w_in, gq, gk, gmat)


def _split_maps(q):
    lane = lax.broadcasted_iota(jnp.int32, q.shape, 1)
    zero = jnp.zeros_like(q)
    return jnp.concatenate([jnp.where(lane < DA_HD, q, zero), jnp.where(lane >= DA_HD, q, zero)], axis=0)


def _flash_init(m_scr, l_scr, acc_scr):
    m_scr[...] = jnp.full(m_scr.shape, MASK_VALUE, F32)
    l_scr[...] = jnp.zeros(l_scr.shape, F32)
    acc_scr[...] = jnp.zeros(acc_scr.shape, F32)


def _flash_step(qq, k, v, bias, m_scr, l_scr, acc_scr):
    s = lax.dot_general(qq, k, (((1,), (1,)), ((), ())), preferred_element_type=F32)
    if bias is not None:
        s = s + bias
    m_prev = m_scr[...]
    m_new = jnp.maximum(m_prev, jnp.max(s, axis=1, keepdims=True))
    alpha = jnp.exp2(m_prev - m_new)
    p = jnp.exp2(s - m_new[:, :1])
    l_scr[...] = alpha * l_scr[...] + jnp.sum(p, axis=1, keepdims=True)
    acc_scr[...] = alpha * acc_scr[...] + jnp.dot(p.astype(BF16), v, preferred_element_type=F32)
    m_scr[...] = m_new


def _flash_step_heads(qqs, ks, vs, biases, m_scr, l_scr, acc_scr):
    nt = (((1,), (1,)), ((), ()))
    ss = [lax.dot_general(qq, k, nt, preferred_element_type=F32) for qq, k in zip(qqs, ks)]
    ss = [s if b is None else s + b for s, b in zip(ss, biases)]
    m_prev = [m_scr[h] for h in range(HEADS)]
    l_prev = [l_scr[h] for h in range(HEADS)]
    acc_prev = [acc_scr[h] for h in range(HEADS)]
    m_new = [jnp.maximum(mp, jnp.max(s, axis=1, keepdims=True)) for mp, s in zip(m_prev, ss)]
    alpha = [jnp.exp2(mp - mn) for mp, mn in zip(m_prev, m_new)]
    ps = [jnp.exp2(s - mn[:, :1]) for s, mn in zip(ss, m_new)]
    pv = [jnp.dot(p.astype(BF16), v, preferred_element_type=F32) for p, v in zip(ps, vs)]
    for h in range(HEADS):
        l_scr[h] = alpha[h] * l_prev[h] + jnp.sum(ps[h], axis=1, keepdims=True)
        acc_scr[h] = alpha[h] * acc_prev[h] + pv[h]
        m_scr[h] = m_new[h]


FLASH_STRIP = 64


def _flash_step_staged(qqs, ks, vs, bias_fn, m_scr, l_scr, acc_scr, s_scr, p_scr, a_scr):
    rows = s_scr.shape[1]
    tk = ks[0].shape[0]
    for h in range(HEADS):
        s_scr[h, :, :tk] = lax.dot_general(qqs[h], ks[h], (((1,), (1,)), ((), ())), preferred_element_type=F32)
        for r0 in range(0, rows, FLASH_STRIP):
            rs = slice(r0, r0 + FLASH_STRIP)
            s = s_scr[h, rs, :tk]
            if bias_fn is not None:
                s = s + bias_fn(h, r0, FLASH_STRIP)
            m_prev = m_scr[h, rs, :]
            m_new = jnp.maximum(m_prev, jnp.max(s, axis=1, keepdims=True))
            a_scr[h, rs, :] = jnp.exp2(m_prev - m_new)
            m_scr[h, rs, :] = m_new
            p_scr[h, rs, :tk] = jnp.exp2(s - jnp.concatenate([m_new] * (tk // HEAD_W), axis=1)).astype(BF16)
        v = vs[h]
        pv = jnp.dot(p_scr[h, :, :tk], jnp.concatenate([v, jnp.ones_like(v)], axis=1),
                     preferred_element_type=F32)
        alpha = a_scr[h]
        acc_scr[h] = alpha * acc_scr[h] + pv[:, :HEAD_W]
        l_scr[h] = alpha * l_scr[h] + pv[:, HEAD_W:]


def _lam(lamv_ref, lam_init):
    lv = lamv_ref[...]
    s1 = jnp.sum(lv[0:1] * lv[1:2], axis=1, keepdims=True)
    s2 = jnp.sum(lv[2:3] * lv[3:4], axis=1, keepdims=True)
    return jnp.exp(s1) - jnp.exp(s2) + lam_init


def _flash_finish(t, lam, lam_init, gda_ref, l_scr, acc_scr):
    o = acc_scr[...] / l_scr[...]
    o = o[:t] - lam * o[t:]
    return _rms(o, gda_ref[...]) * (1.0 - lam_init)


def _retention_step(qr, kr, vr, s_prev, dmat, dq, dk, dc):
    att = lax.dot_general(qr, kr, (((1,), (1,)), ((), ())), preferred_element_type=F32) * dmat
    o = jnp.dot(att.astype(BF16), vr, preferred_element_type=F32)
    o = o + jnp.dot(qr, s_prev.astype(BF16), preferred_element_type=F32) * dq
    kdec = (kr.astype(F32) * dk).astype(BF16)
    s_new = s_prev * dc + lax.dot_general(kdec, vr, (((0,), (0,)), ((), ())), preferred_element_type=F32)
    return o, s_new


def _decay_tables(c):
    log_g = jnp.log1p(-jnp.power(2.0, -5.0 - jnp.arange(HEADS, dtype=F32)))
    idx = jnp.arange(c, dtype=F32)
    diff = idx[:, None] - idx[None, :]
    dmat = jnp.where(diff >= 0, jnp.exp(log_g[:, None, None] * jnp.maximum(diff, 0.0)), 0.0)
    dq = jnp.broadcast_to(jnp.exp(log_g[:, None] * (idx + 1.0)[None, :])[:, :, None], (HEADS, c, HEAD_W))
    dk = jnp.broadcast_to(jnp.exp(log_g[:, None] * (c - 1.0 - idx)[None, :])[:, :, None], (HEADS, c, HEAD_W))
    dc = jnp.broadcast_to(jnp.exp(log_g * c)[:, None, None], (HEADS, 8, HEAD_W))
    return dmat.astype(F32), dq.astype(F32), dk.astype(F32), dc.astype(F32)


def _head(h):
    return slice(h * HEAD_W, (h + 1) * HEAD_W)


def _mixer_prompt_kernel(lam_init, q_ref, k_ref, v_ref, bias_ref, qr_ref, kr_ref, vr_ref, sg_ref,
                         dmat_ref, dq_ref, dk_ref, dc_ref, s0_ref, lamv_ref, gda_ref, grt_ref,
                         od_ref, or_ref, sout_ref, m_scr, l_scr, acc_scr, sc_scr, p_scr, a_scr, s_scr):
    qb = pl.program_id(1)
    t = q_ref.shape[0]
    qq = [_split_maps(q_ref[:, _head(h)]) for h in range(HEADS)]
    _flash_init(m_scr, l_scr, acc_scr)

    def key_step(kb, nblk, bias_cols):
        off = pl.multiple_of(kb * t, t)
        bias_fn = None if bias_cols is None else (lambda h, r0, n: bias_ref[h, 0, r0 % t:r0 % t + n, bias_cols])
        _flash_step_staged(qq, [k_ref[pl.ds(off, nblk * t), _head(h)] for h in range(HEADS)],
                           [v_ref[pl.ds(off, nblk * t), _head(h)] for h in range(HEADS)], bias_fn,
                           m_scr, l_scr, acc_scr, sc_scr, p_scr, a_scr)

    n_far = jnp.maximum(qb - 1, 0)

    def far_body(i, carry):
        key_step(2 * i, 2, None)
        return carry

    lax.fori_loop(0, n_far // 2, far_body, 0)

    @pl.when(n_far % 2 == 1)
    def _():
        key_step(n_far - 1, 1, None)

    @pl.when(qb >= 1)
    def _():
        key_step(qb - 1, 2, slice(0, 2 * t))

    @pl.when(qb == 0)
    def _():
        key_step(0, 1, slice(t, 2 * t))

    lam = _lam(lamv_ref, lam_init)

    @pl.when(qb == 0)
    def _():
        s_scr[...] = s0_ref[0]

    for h in range(HEADS):
        od_ref[:, _head(h)] = _flash_finish(t, lam, lam_init, gda_ref, l_scr.at[h], acc_scr.at[h]).astype(BF16)
        o, s_new = _retention_step(qr_ref[:, _head(h)], kr_ref[:, _head(h)], vr_ref[:, _head(h)], s_scr[h],
                                   dmat_ref[h], dq_ref[h], dk_ref[h], dc_ref[h, 0:1])
        s_scr[h] = s_new
        sout_ref[0, h] = s_new
        or_ref[:, _head(h)] = (_rms(o, grt_ref[...]) * sg_ref[:, _head(h)].astype(F32)).astype(BF16)


def _mixer_prompt(lam_init, b, t, q, kb, vb, bias, qr, kr, vr, sg, s0, lamv, gda, grt):
    tq = ATTN_BLOCK
    nq = t // tq
    n = b * t
    tables = _decay_tables(tq)
    tile = pl.BlockSpec((tq, SEG), lambda bi, qb: (bi * nq + qb, 0))
    seq = pl.BlockSpec((t, SEG), lambda bi, qb: (bi, 0))
    state = pl.BlockSpec((1, HEADS, RT_KD, HEAD_W), lambda bi, qb: (bi, 0, 0, 0))
    return pl.pallas_call(
        functools.partial(_mixer_prompt_kernel, lam_init),
        grid=(b, nq),
        in_specs=[tile, seq, seq, _const_spec(bias.shape), tile, tile, tile, tile]
                 + [_const_spec(a.shape) for a in tables]
                 + [state, _const_spec(lamv.shape), _const_spec(gda.shape), _const_spec(grt.shape)],
        out_specs=[tile, tile, state],
        out_shape=[jax.ShapeDtypeStruct((n, SEG), BF16), jax.ShapeDtypeStruct((n, SEG), BF16),
                   jax.ShapeDtypeStruct((b, HEADS, RT_KD, HEAD_W), F32)],
        scratch_shapes=[pltpu.VMEM((HEADS, 2 * tq, HEAD_W), F32), pltpu.VMEM((HEADS, 2 * tq, HEAD_W), F32),
                        pltpu.VMEM((HEADS, 2 * tq, HEAD_W), F32), pltpu.VMEM((HEADS, 2 * tq, 2 * tq), F32),
                        pltpu.VMEM((HEADS, 2 * tq, 2 * tq), BF16), pltpu.VMEM((HEADS, 2 * tq, HEAD_W), F32),
                        pltpu.VMEM((HEADS, RT_KD, HEAD_W), F32)],
        compiler_params=_cparams("arbitrary", "arbitrary"),
        name="mixer_prompt",
    )(q, kb, vb, bias, qr, kr, vr, sg, *tables, s0, lamv, gda, grt)


CACHE_CHUNK = 1024


def _mixer_sample_kernel(lam_init, q_ref, ck_ref, cv_ref, kn_ref, vn_ref, bnear_ref, bnew_ref,
                         qr_ref, kr_ref, vr_ref, sg_ref, dmat_ref, dq_ref, dk_ref, dc_ref, s0_ref,
                         lamv_ref, gda_ref, grt_ref, od_ref, or_ref, sout_ref, m_scr, l_scr, acc_scr):
    c = pl.program_id(1)
    last = pl.num_programs(1) - 1
    t = q_ref.shape[0]
    qq = [_split_maps(q_ref[:, _head(h)]) for h in range(HEADS)]

    @pl.when(c == 0)
    def _():
        _flash_init(m_scr, l_scr, acc_scr)

    def cache_keys(bias_ref):
        rows = [pl.ds(h, CACHE_CHUNK, stride=HEADS) for h in range(HEADS)]
        _flash_step_heads(
            qq, [ck_ref[r, :].astype(BF16) for r in rows], [cv_ref[r, :].astype(BF16) for r in rows],
            [None if bias_ref is None else jnp.concatenate([bias_ref[h, 0]] * 2, axis=0) for h in range(HEADS)],
            m_scr, l_scr, acc_scr)

    @pl.when(c < last)
    def _():
        cache_keys(None)

    @pl.when(c == last)
    def _():
        cache_keys(bnear_ref)
        lam = _lam(lamv_ref, lam_init)
        _flash_step_heads(
            qq, [kn_ref[:, _head(h)] for h in range(HEADS)], [vn_ref[:, _head(h)] for h in range(HEADS)],
            [jnp.concatenate([bnew_ref[h, 0]] * 2, axis=0) for h in range(HEADS)], m_scr, l_scr, acc_scr)
        for h in range(HEADS):
            od_ref[:, _head(h)] = _flash_finish(t, lam, lam_init, gda_ref, l_scr.at[h], acc_scr.at[h]).astype(BF16)
            o, s_new = _retention_step(qr_ref[:, _head(h)], kr_ref[:, _head(h)], vr_ref[:, _head(h)], s0_ref[0, h],
                                       dmat_ref[h], dq_ref[h], dk_ref[h], dc_ref[h, 0:1])
            sout_ref[0, h] = s_new
            or_ref[:, _head(h)] = (_rms(o, grt_ref[...]) * sg_ref[:, _head(h)].astype(F32)).astype(BF16)


def _mixer_sample(lam_init, layer, b, t, past, q, cache_k, cache_v, kn, vn, bnear, bnew, qr, kr, vr, sg, s0,
                  lamv, gda, grt):
    n = b * t
    assert past % CACHE_CHUNK == 0
    tables = _decay_tables(t)
    tile = pl.BlockSpec((t, SEG), lambda bi, c: (bi, 0))
    chunks = past // CACHE_CHUNK
    cache_k, cache_v = cache_k.reshape(-1, HEAD_W), cache_v.reshape(-1, HEAD_W)
    cache = pl.BlockSpec((CACHE_CHUNK * HEADS, HEAD_W), lambda bi, c: ((layer * b + bi) * chunks + c, 0))
    state = pl.BlockSpec((1, HEADS, RT_KD, HEAD_W), lambda bi, c: (bi, 0, 0, 0))
    consts = [bnear, bnew]
    return pl.pallas_call(
        functools.partial(_mixer_sample_kernel, lam_init),
        grid=(b, past // CACHE_CHUNK),
        in_specs=[tile, cache, cache, tile, tile] + [_const_spec(a.shape) for a in consts]
                 + [tile, tile, tile, tile] + [_const_spec(a.shape) for a in tables]
                 + [state, _const_spec(lamv.shape), _const_spec(gda.shape), _const_spec(grt.shape)],
        out_specs=[tile, tile, state],
        out_shape=[jax.ShapeDtypeStruct((n, SEG), BF16), jax.ShapeDtypeStruct((n, SEG), BF16),
                   jax.ShapeDtypeStruct((b, HEADS, RT_KD, HEAD_W), F32)],
        scratch_shapes=[pltpu.VMEM((HEADS, 2 * t, HEAD_W), F32), pltpu.VMEM((HEADS, 2 * t, HEAD_W), F32),
                        pltpu.VMEM((HEADS, 2 * t, HEAD_W), F32)],
        compiler_params=_cparams("arbitrary", "arbitrary"),
        name="mixer_sample",
    )(q, cache_k, cache_v, kn, vn, bnear, bnew, qr, kr, vr, sg, *tables, s0, lamv, gda, grt)


FFN_CHUNK = 512


def _channel_kernel(nseq, rows, tiles_per_seq, x_ref, od_ref, or_ref, sd_ref, sr_ref, p_ref, st_ref,
                    wbd_ref, wbr_ref, wo_ref, gffn_ref, wg_ref, wu_ref, cw_ref, cb_ref, wd_ref,
                    gpe_ref, wpe_ref, wpg_ref, y_ref, cn_ref, gbuf):
    i = pl.program_id(0)
    dot = functools.partial(jnp.dot, preferred_element_type=F32)
    mix = (sd_ref[...].astype(F32) * dot(od_ref[...], wbd_ref[...])
           + sr_ref[...].astype(F32) * dot(or_ref[...], wbr_ref[...]))
    h1 = x_ref[...] + dot(mix.astype(BF16), wo_ref[...])

    f = _rms(h1, gffn_ref[...]).astype(BF16)
    pe = dot(p_ref[...].astype(BF16), wpe_ref[...])
    if tiles_per_seq > 1:
        @pl.when(i % tiles_per_seq == 0)
        def _():
            gbuf[6:8, :] = st_ref[0]
    stride = rows + 8
    h2 = h1
    chunks = [slice(c0, min(c0 + FFN_CHUNK, D_FF)) for c0 in range(0, D_FF, FFN_CHUNK)]
    up = lambda cs: (dot(f, wg_ref[:, cs]), dot(f, wu_ref[:, cs]))
    nxt = up(chunks[0])
    for ci, cs in enumerate(chunks):
        g, u = nxt
        if ci + 1 < len(chunks):
            nxt = up(chunks[ci + 1])
        cw = cw_ref[:, cs]
        pieces = []
        for j in range(nseq):
            base = j * stride
            if tiles_per_seq == 1:
                gbuf[base + 6:base + 8, cs] = st_ref[j, :, cs]
            gj = g[j * rows:(j + 1) * rows]
            gbuf[base + 8:base + 8 + rows, cs] = gj
            pieces.append(cb_ref[:, cs] + gbuf[base + 6:base + 6 + rows, cs] * cw[0:1]
                          + gbuf[base + 7:base + 7 + rows, cs] * cw[1:2] + gj * cw[2:3])
            last2 = gbuf[base + rows + 6:base + rows + 8, cs]
            cn_ref[j, :, cs] = last2
            if tiles_per_seq > 1:
                gbuf[6:8, cs] = last2
        gc = pieces[0] if nseq == 1 else jnp.concatenate(pieces, axis=0)
        h2 = h2 + dot((jax.nn.gelu(gc) * u).astype(BF16), wd_ref[cs, :])

    gate = jax.nn.sigmoid(dot(_rms(h2, gpe_ref[...]).astype(BF16), wpg_ref[...]))
    y_ref[...] = h2 + pe * gate


def _channel(b, t, tm, x2, od, orr, sd, sr, p2, st, w):
    n = b * t
    rows = min(tm, t)
    nseq = tm // rows
    tiles_per_seq = t // rows
    row = lambda wd: pl.BlockSpec((tm, wd), lambda i: (i, 0))
    if tiles_per_seq == 1:
        st_spec = pl.BlockSpec((nseq, CONV_W - 1, D_FF), lambda i: (i, 0, 0))
    else:
        st_spec = pl.BlockSpec((1, CONV_W - 1, D_FF), lambda i: (i // tiles_per_seq, 0, 0))
    weights = [w["w_bd"], w["w_br"], w["w_o"], w["g_ffn"], w["w_g"], w["w_u"], w["conv_w"], w["conv_b"],
               w["w_d"], w["g_pe"], w["w_pe"], w["w_pg"]]
    return pl.pallas_call(
        functools.partial(_channel_kernel, nseq, rows, tiles_per_seq),
        grid=(n // tm,),
        in_specs=[row(D_MODEL), row(SEG), row(SEG), row(D_MODEL), row(D_MODEL), row(PE_DIM), st_spec]
                 + [_const_spec(a.shape) for a in weights],
        out_specs=[row(D_MODEL), st_spec],
        out_shape=[jax.ShapeDtypeStruct((n, D_MODEL), F32), jax.ShapeDtypeStruct((b, CONV_W - 1, D_FF), F32)],
        scratch_shapes=[pltpu.VMEM((nseq * (rows + 8), D_FF), F32)],
        compiler_params=_cparams("arbitrary"),
        name="channel",
    )(x2, od, orr, sd, sr, p2, st, *weights)


def _rope_tables(pos):
    half = HEAD_W // 2
    inv = jnp.power(ROPE_BASE, -jnp.arange(half, dtype=F32) / half)
    ang = pos.astype(F32)[:, None] * inv[None, :]
    cos, sin = jnp.cos(ang), jnp.sin(ang)
    return jnp.concatenate([cos, cos], axis=1), jnp.concatenate([-sin, sin], axis=1)


def _prep_weights(i, g_mix, w_in, g_q, g_k, lam_q1, lam_k1, lam_q2, lam_k2, g_da, g_rt, w_bd, w_br, w_o,
                  g_ffn, w_g, w_u, conv_w, conv_b, w_d, g_pe, w_pe, w_pg):
    grp = np.arange(256) // DA_HD
    return dict(
        g_mix=g_mix[i][None], w_in=w_in[i].astype(BF16),
        g_q=jnp.tile(g_q[i], SEG // DA_HD)[None], g_k=jnp.tile(g_k[i], SEG // DA_HD)[None],
        gmat=jnp.asarray((grp[:, None] == grp[None, :]) / DA_HD, BF16),
        lamv=jnp.stack([lam_q1[i], lam_k1[i], lam_q2[i], lam_k2[i]]),
        g_da=g_da[i][None], g_rt=g_rt[i][None],
        w_bd=w_bd[i].astype(BF16), w_br=w_br[i].astype(BF16), w_o=w_o[i].astype(BF16),
        g_ffn=g_ffn[i][None], w_g=w_g[i].astype(BF16), w_u=w_u[i].astype(BF16),
        conv_w=conv_w[i], conv_b=conv_b[i][None], w_d=w_d[i].astype(BF16),
        g_pe=g_pe[i][None], w_pe=w_pe[i].astype(BF16), w_pg=w_pg[i].astype(BF16))


def _layer(h, pe, layer, cache_k, cache_v, s_ret, s_conv, rel_bias, lam_init, w):
    b, t, _ = h.shape
    n = b * t
    past = 0 if cache_k is None else cache_k.shape[2]
    x2 = h.reshape(n, D_MODEL)
    qpos = past + jnp.arange(t, dtype=jnp.int32)

    tm_in = min(512, n)
    cos_t, sin_t = _rope_tables(qpos)
    if t < tm_in:
        cos_t, sin_t = jnp.tile(cos_t, (tm_in // t, 1)), jnp.tile(sin_t, (tm_in // t, 1))
    q, k32, kb, v32, vb, qr, kr, vr, sg, sd, sr = _inproj(
        x2, cos_t, sin_t, w["g_mix"], w["w_in"], w["g_q"], w["g_k"], w["gmat"], tm_in)

    if cache_k is None:
        tq = ATTN_BLOCK
        assert t % tq == 0 and tq % CHUNK == 0 and tq >= MAX_DIST
        bias = _bias_tiles(rel_bias, tq + jnp.arange(tq, dtype=jnp.int32)[None],
                           jnp.arange(2 * tq, dtype=jnp.int32)[None])
        s0 = jnp.zeros((b, HEADS, RT_KD, HEAD_W), F32)
        od, orr, s_new = _mixer_prompt(lam_init, b, t, q, kb, vb, bias, qr, kr, vr, sg, s0,
                                       w["lamv"], w["g_da"], w["g_rt"])
        st = jnp.zeros((b, CONV_W - 1, D_FF), F32)
    else:
        assert past % CHUNK == 0 and t <= CHUNK and CACHE_CHUNK >= MAX_DIST
        near = jnp.arange(past - CACHE_CHUNK, past, dtype=jnp.int32)
        bnear = _bias_tiles(rel_bias, qpos[None], near[None])
        bnew = _bias_tiles(rel_bias, qpos[None], qpos[None])
        od, orr, s_new = _mixer_sample(lam_init, layer, b, t, past, q, cache_k, cache_v, kb, vb, bnear, bnew,
                                       qr, kr, vr, sg, s_ret, w["lamv"], w["g_da"], w["g_rt"])
        st = s_conv

    tm_ch = min(256, n)
    y, conv_new = _channel(b, t, tm_ch, x2, od, orr, sd, sr, pe.reshape(n, PE_DIM), st, w)
    return (y.reshape(b, t, D_MODEL), k32.reshape(b, t, HEADS, HEAD_W), v32.reshape(b, t, HEADS, HEAD_W),
            s_new, conv_new)


def kernel(x_prompt, x_sample, p_prompt, p_sample, cache_k, cache_v, state_ret, state_conv, rel_bias, g_mix, w_in, g_q, g_k, lam_q1, lam_k1, lam_q2, lam_k2, g_da, g_rt, w_bd, w_br, w_o, g_ffn, w_g, w_u, conv_w, conv_b, w_d, g_pe, w_pe, w_pg):
    depth = w_in.shape[0]
    hp, hs = x_prompt, x_sample
    outs = [[] for _ in range(8)]
    for i in range(depth):
        lam_init = 0.8 - 0.6 * math.exp(-0.3 * i)
        w = _prep_weights(i, g_mix, w_in, g_q, g_k, lam_q1, lam_k1, lam_q2, lam_k2, g_da, g_rt, w_bd, w_br,
                          w_o, g_ffn, w_g, w_u, conv_w, conv_b, w_d, g_pe, w_pe, w_pg)
        hp, kp, vp, rp, cp = _layer(hp, p_prompt[i], i, None, None, None, None, rel_bias, lam_init, w)
        hs, ks, vs, rs, cs = _layer(hs, p_sample[i], i, cache_k, cache_v, state_ret[i], state_conv[i],
                                    rel_bias, lam_init, w)
        for lst, val in zip(outs, (kp, vp, rp, cp, ks, vs, rs, cs)):
            lst.append(val)
    return (hp, hs) + tuple(jnp.stack(o) for o in outs)
```

```python
import functools
import math

import numpy as np
import jax
import jax.numpy as jnp
from jax import lax
from jax.experimental import pallas as pl
from jax.experimental.pallas import tpu as pltpu

D_MODEL = 1024
CHUNK = 64
PE_DIM = 256
HEADS = 4
HEAD_W = 128
DA_HD = 64
RT_KD = 128
D_FF = 2816
CONV_W = 3
N_BUCKETS = 32
MAX_DIST = 128
EPS = 1e-6
ROPE_BASE = 10000.0
SEG = HEADS * HEAD_W
D_IN = 7 * SEG + 2 * D_MODEL

MASK_VALUE = -1e30
V7X_VMEM_LIMIT = 56 * 1024 * 1024
ATTN_BLOCK = 256
F32 = jnp.float32
BF16 = jnp.bfloat16


def _cparams(*sem):
    return pltpu.CompilerParams(dimension_semantics=sem, vmem_limit_bytes=V7X_VMEM_LIMIT)


def _const_spec(shape):
    nd = len(shape)
    return pl.BlockSpec(shape, lambda *_: (0,) * nd, pipeline_mode=pl.Buffered(1))


def _rms(x, g):
    return x * lax.rsqrt(jnp.mean(x * x, axis=-1, keepdims=True) + EPS) * g


def _t5_bucket(rel):
    nb = N_BUCKETS // 2
    ret = jnp.where(rel > 0, nb, 0)
    n = jnp.abs(rel)
    max_exact = nb // 2
    nf = jnp.maximum(n, 1).astype(F32)
    large = max_exact + (jnp.log(nf / max_exact) / math.log(MAX_DIST / max_exact) * (nb - max_exact)).astype(jnp.int32)
    large = jnp.minimum(large, nb - 1)
    return ret + jnp.where(n < max_exact, n, large)


FAR_BUCKET = N_BUCKETS // 2 - 1
LOG2E = math.log2(math.e)


def _bias_kernel(idx_ref, madd_ref, rb_ref, out_ref):
    h = pl.program_id(0)
    idx = idx_ref[...]
    bias = jnp.zeros(idx.shape, F32)
    for b in range(N_BUCKETS):
        bias = jnp.where(idx == b, rb_ref[b, h], bias)
    out_ref[0] = (bias - rb_ref[FAR_BUCKET, h]) * LOG2E + madd_ref[...]


def _bias_tiles(rel_bias, qpos, kpos):
    rel = kpos[:, None, :] - qpos[:, :, None]
    idx = _t5_bucket(rel).astype(jnp.int32)
    madd = jnp.where((kpos[:, None, :] // CHUNK) <= (qpos[:, :, None] // CHUNK), 0.0, MASK_VALUE).astype(F32)
    n, r, c = idx.shape
    return pl.pallas_call(
        _bias_kernel,
        grid=(HEADS,),
        in_specs=[pl.BlockSpec((n, r, c), lambda h: (0, 0, 0)),
                  pl.BlockSpec((n, r, c), lambda h: (0, 0, 0)),
                  pl.BlockSpec(memory_space=pltpu.SMEM)],
        out_specs=pl.BlockSpec((1, n, r, c), lambda h: (h, 0, 0, 0)),
        out_shape=jax.ShapeDtypeStruct((HEADS, n, r, c), F32),
        compiler_params=_cparams("arbitrary"),
        name="bias_tiles",
    )(idx, madd, rel_bias)


def _inproj_kernel(x_ref, cos_ref, sin_ref, gmix_ref, w_ref, gq_ref, gk_ref, gmat_ref,
                   q_ref, k32_ref, kb_ref, v32_ref, vb_ref, qr_ref, kr_ref, vr_ref, sg_ref, sd_ref, sr_ref):
    a = _rms(x_ref[...], gmix_ref[...]).astype(BF16)

    def seg(c0, c1):
        return jnp.dot(a, w_ref[:, c0:c1], preferred_element_type=F32)

    def group_norm(z, g):
        sq = (z * z).astype(BF16)
        outs = []
        for c in range(0, SEG, 256):
            ms = jnp.dot(sq[:, c:c + 256], gmat_ref[...], preferred_element_type=F32)
            outs.append(z[:, c:c + 256] * lax.rsqrt(ms + EPS))
        return jnp.concatenate(outs, axis=1) * g

    def rotary(z):
        cos, sin = cos_ref[...], sin_ref[...]
        outs = []
        for h in range(HEADS):
            zs = z[:, h * HEAD_W:(h + 1) * HEAD_W]
            outs.append(zs * cos + pltpu.roll(zs, HEAD_W // 2, 1) * sin)
        return jnp.concatenate(outs, axis=1)

    def by_head(z, out32_ref):
        for h in range(HEADS):
            out32_ref[pl.ds(h, z.shape[0], stride=HEADS), :] = z[:, h * HEAD_W:(h + 1) * HEAD_W]

    def put_q(z):
        q_ref[...] = (group_norm(z, gq_ref[...]) * (DA_HD ** -0.5 * LOG2E)).astype(BF16)

    def put_k(z):
        k = group_norm(z, gk_ref[...])
        kb_ref[...] = k.astype(BF16)
        by_head(k, k32_ref)

    def put_v(z):
        vb_ref[...] = z.astype(BF16)
        by_head(z, v32_ref)

    def put_qr(z):
        qr_ref[...] = rotary(z).astype(BF16)

    def put_kr(z):
        kr_ref[...] = (rotary(z) * RT_KD ** -0.5).astype(BF16)

    def put_vr(z):
        vr_ref[...] = z.astype(BF16)

    def put_sg(z):
        sg_ref[...] = (z * jax.nn.sigmoid(z)).astype(BF16)

    def put_sd(z):
        sd_ref[...] = jax.nn.sigmoid(z).astype(BF16)

    def put_sr(z):
        sr_ref[...] = jax.nn.sigmoid(z).astype(BF16)

    puts = [put_q, put_k, put_v, put_qr, put_kr, put_vr, put_sg, put_sd, put_sr]
    cols = [0, SEG, 2 * SEG, 3 * SEG, 4 * SEG, 5 * SEG, 6 * SEG, 7 * SEG, 7 * SEG + D_MODEL, D_IN]
    z_next = seg(cols[0], cols[1])
    for i, put in enumerate(puts):
        z = z_next
        if i + 1 < len(puts):
            z_next = seg(cols[i + 1], cols[i + 2])
        put(z)


def _inproj(x2, cos_t, sin_t, gmix, w_in, gq, gk, gmat, tm):
    n = x2.shape[0]
    nt = cos_t.shape[0] // tm
    row = lambda w: pl.BlockSpec((tm, w), lambda i: (i, 0))
    tab = pl.BlockSpec((tm, HEAD_W), lambda i: (i % nt, 0))
    out_w = [(SEG, BF16), (None, F32), (SEG, BF16), (None, F32), (SEG, BF16), (SEG, BF16), (SEG, BF16),
             (SEG, BF16), (SEG, BF16), (D_MODEL, BF16), (D_MODEL, BF16)]
    by_head = pl.BlockSpec((tm * HEADS, HEAD_W), lambda i: (i, 0))
    return pl.pallas_call(
        _inproj_kernel,
        grid=(n // tm,),
        in_specs=[row(D_MODEL), tab, tab, _const_spec((1, D_MODEL)), _const_spec((D_MODEL, D_IN)),
                  _const_spec((1, SEG)), _const_spec((1, SEG)), _const_spec((256, 256))],
        out_specs=[by_head if w is None else row(w) for w, _ in out_w],
        out_shape=[jax.ShapeDtypeStruct((n * HEADS, HEAD_W) if w is None else (n, w), dt) for w, dt in out_w],
        compiler_params=_cparams("arbitrary"),
        name="inproj",
    )(x2, cos_t, sin_t, gmix, w_in, gq, gk, gmat)


def _split_maps(q):
    lane = lax.broadcasted_iota(jnp.int32, q.shape, 1)
    zero = jnp.zeros_like(q)
    return jnp.concatenate([jnp.where(lane < DA_HD, q, zero), jnp.where(lane >= DA_HD, q, zero)], axis=0)


def _flash_init(m_scr, l_scr, acc_scr):
    m_scr[...] = jnp.full(m_scr.shape, MASK_VALUE, F32)
    l_scr[...] = jnp.zeros(l_scr.shape, F32)
    acc_scr[...] = jnp.zeros(acc_scr.shape, F32)


def _flash_step(qq, k, v, bias, m_scr, l_scr, acc_scr):
    s = lax.dot_general(qq, k, (((1,), (1,)), ((), ())), preferred_element_type=F32)
    if bias is not None:
        s = s + bias
    m_prev = m_scr[...]
    m_new = jnp.maximum(m_prev, jnp.max(s, axis=1, keepdims=True))
    alpha = jnp.exp2(m_prev - m_new)
    p = jnp.exp2(s - m_new[:, :1])
    l_scr[...] = alpha * l_scr[...] + jnp.sum(p, axis=1, keepdims=True)
    acc_scr[...] = alpha * acc_scr[...] + jnp.dot(p.astype(BF16), v, preferred_element_type=F32)
    m_scr[...] = m_new


def _flash_step_units(qqs, ks, vs, biases, m_scr, l_scr, acc_scr):
    nt = (((1,), (1,)), ((), ()))
    units = range(len(qqs))
    ss = [lax.dot_general(qq, k, nt, preferred_element_type=F32) for qq, k in zip(qqs, ks)]
    ss = [s if b is None else s + b for s, b in zip(ss, biases)]
    m_prev = [m_scr[u] for u in units]
    l_prev = [l_scr[u] for u in units]
    acc_prev = [acc_scr[u] for u in units]
    m_new = [jnp.maximum(mp, jnp.max(s, axis=1, keepdims=True)) for mp, s in zip(m_prev, ss)]
    alpha = [jnp.exp2(mp - mn) for mp, mn in zip(m_prev, m_new)]
    ps = [jnp.exp2(s - mn[:, :1]) for s, mn in zip(ss, m_new)]
    pv = [jnp.dot(p.astype(BF16), v, preferred_element_type=F32) for p, v in zip(ps, vs)]
    for u in units:
        l_scr[u] = alpha[u] * l_prev[u] + jnp.sum(ps[u], axis=1, keepdims=True)
        acc_scr[u] = alpha[u] * acc_prev[u] + pv[u]
        m_scr[u] = m_new[u]


FLASH_STRIP = 64


def _flash_step_staged(qqs, ks, vs, bias_fn, m_scr, l_scr, acc_scr, s_scr, p_scr, a_scr):
    rows = s_scr.shape[1]
    tk = ks[0].shape[0]
    for h in range(HEADS):
        s_scr[h, :, :tk] = lax.dot_general(qqs[h], ks[h], (((1,), (1,)), ((), ())), preferred_element_type=F32)
        for r0 in range(0, rows, FLASH_STRIP):
            rs = slice(r0, r0 + FLASH_STRIP)
            s = s_scr[h, rs, :tk]
            if bias_fn is not None:
                s = s + bias_fn(h, r0, FLASH_STRIP)
            m_prev = m_scr[h, rs, :]
            m_new = jnp.maximum(m_prev, jnp.max(s, axis=1, keepdims=True))
            a_scr[h, rs, :] = jnp.exp2(m_prev - m_new)
            m_scr[h, rs, :] = m_new
            p_scr[h, rs, :tk] = jnp.exp2(s - jnp.concatenate([m_new] * (tk // HEAD_W), axis=1)).astype(BF16)
        v = vs[h]
        pv = jnp.dot(p_scr[h, :, :tk], jnp.concatenate([v, jnp.ones_like(v)], axis=1),
                     preferred_element_type=F32)
        alpha = a_scr[h]
        acc_scr[h] = alpha * acc_scr[h] + pv[:, :HEAD_W]
        l_scr[h] = alpha * l_scr[h] + pv[:, HEAD_W:]


def _lam(lamv_ref, lam_init):
    lv = lamv_ref[...]
    s1 = jnp.sum(lv[0:1] * lv[1:2], axis=1, keepdims=True)
    s2 = jnp.sum(lv[2:3] * lv[3:4], axis=1, keepdims=True)
    return jnp.exp(s1) - jnp.exp(s2) + lam_init


def _flash_finish(t, lam, lam_init, gda_ref, l_scr, acc_scr):
    o = acc_scr[...] / l_scr[...]
    o = o[:t] - lam * o[t:]
    return _rms(o, gda_ref[...]) * (1.0 - lam_init)


def _retention_step(qr, kr, vr, s_prev, dmat, dq, dk, dc):
    att = lax.dot_general(qr, kr, (((1,), (1,)), ((), ())), preferred_element_type=F32) * dmat
    o = jnp.dot(att.astype(BF16), vr, preferred_element_type=F32)
    o = o + jnp.dot(qr, s_prev.astype(BF16), preferred_element_type=F32) * dq
    kdec = (kr.astype(F32) * dk).astype(BF16)
    s_new = s_prev * dc + lax.dot_general(kdec, vr, (((0,), (0,)), ((), ())), preferred_element_type=F32)
    return o, s_new


def _decay_tables(c):
    log_g = jnp.log1p(-jnp.power(2.0, -5.0 - jnp.arange(HEADS, dtype=F32)))
    idx = jnp.arange(c, dtype=F32)
    diff = idx[:, None] - idx[None, :]
    dmat = jnp.where(diff >= 0, jnp.exp(log_g[:, None, None] * jnp.maximum(diff, 0.0)), 0.0)
    dq = jnp.broadcast_to(jnp.exp(log_g[:, None] * (idx + 1.0)[None, :])[:, :, None], (HEADS, c, HEAD_W))
    dk = jnp.broadcast_to(jnp.exp(log_g[:, None] * (c - 1.0 - idx)[None, :])[:, :, None], (HEADS, c, HEAD_W))
    dc = jnp.broadcast_to(jnp.exp(log_g * c)[:, None, None], (HEADS, 8, HEAD_W))
    return dmat.astype(F32), dq.astype(F32), dk.astype(F32), dc.astype(F32)


def _head(h):
    return slice(h * HEAD_W, (h + 1) * HEAD_W)


SEQ_GROUP = 2


def _mixer_prompt_kernel(lam_init, q_ref, k_ref, v_ref, bias_ref, qr_ref, kr_ref, vr_ref, sg_ref,
                         dmat_ref, dq_ref, dk_ref, dc_ref, s0_ref, lamv_ref, gda_ref, grt_ref,
                         od_ref, or_ref, sout_ref, m_scr, l_scr, acc_scr, sc_scr, p_scr, a_scr, s_scr):
    qb = pl.program_id(1)
    ng, t = q_ref.shape[0], q_ref.shape[1]
    qq = [[_split_maps(q_ref[g, :, _head(h)]) for h in range(HEADS)] for g in range(ng)]
    _flash_init(m_scr, l_scr, acc_scr)

    def key_step(kb, nblk, bias_cols):
        off = pl.multiple_of(kb * t, t)
        bias_fn = None if bias_cols is None else (lambda h, r0, n: bias_ref[h, 0, r0 % t:r0 % t + n, bias_cols])
        for g in range(ng):
            _flash_step_staged(qq[g], [k_ref[g, pl.ds(off, nblk * t), _head(h)] for h in range(HEADS)],
                               [v_ref[g, pl.ds(off, nblk * t), _head(h)] for h in range(HEADS)], bias_fn,
                               m_scr.at[g], l_scr.at[g], acc_scr.at[g], sc_scr.at[g], p_scr.at[g], a_scr.at[g])

    n_far = jnp.maximum(qb - 1, 0)

    def far_body(i, carry):
        key_step(2 * i, 2, None)
        return carry

    lax.fori_loop(0, n_far // 2, far_body, 0)

    @pl.when(n_far % 2 == 1)
    def _():
        key_step(n_far - 1, 1, None)

    @pl.when(qb >= 1)
    def _():
        key_step(qb - 1, 2, slice(0, 2 * t))

    @pl.when(qb == 0)
    def _():
        key_step(0, 1, slice(t, 2 * t))

    lam = _lam(lamv_ref, lam_init)

    @pl.when(qb == 0)
    def _():
        s_scr[...] = s0_ref[...]

    for g in range(ng):
        for h in range(HEADS):
            od_ref[g, :, _head(h)] = _flash_finish(
                t, lam, lam_init, gda_ref, l_scr.at[g, h], acc_scr.at[g, h]).astype(BF16)
            o, s_new = _retention_step(qr_ref[g, :, _head(h)], kr_ref[g, :, _head(h)], vr_ref[g, :, _head(h)],
                                       s_scr[g, h], dmat_ref[h], dq_ref[h], dk_ref[h], dc_ref[h, 0:1])
            s_scr[g, h] = s_new
            sout_ref[g, h] = s_new
            or_ref[g, :, _head(h)] = (_rms(o, grt_ref[...]) * sg_ref[g, :, _head(h)].astype(F32)).astype(BF16)


def _mixer_prompt(lam_init, b, t, q, kb, vb, bias, qr, kr, vr, sg, s0, lamv, gda, grt):
    tq = ATTN_BLOCK
    nq = t // tq
    ng = SEQ_GROUP if b % SEQ_GROUP == 0 else 1
    tables = _decay_tables(tq)
    by_seq = lambda a: a.reshape(b, t, SEG)
    tile = pl.BlockSpec((ng, tq, SEG), lambda bi, qb: (bi, qb, 0))
    seq = pl.BlockSpec((ng, t, SEG), lambda bi, qb: (bi, 0, 0))
    state = pl.BlockSpec((ng, HEADS, RT_KD, HEAD_W), lambda bi, qb: (bi, 0, 0, 0))
    stat = pltpu.VMEM((ng, HEADS, 2 * tq, HEAD_W), F32)
    od, orr, s_new = pl.pallas_call(
        functools.partial(_mixer_prompt_kernel, lam_init),
        grid=(b // ng, nq),
        in_specs=[tile, seq, seq, _const_spec(bias.shape), tile, tile, tile, tile]
                 + [_const_spec(a.shape) for a in tables]
                 + [state, _const_spec(lamv.shape), _const_spec(gda.shape), _const_spec(grt.shape)],
        out_specs=[tile, tile, state],
        out_shape=[jax.ShapeDtypeStruct((b, t, SEG), BF16), jax.ShapeDtypeStruct((b, t, SEG), BF16),
                   jax.ShapeDtypeStruct((b, HEADS, RT_KD, HEAD_W), F32)],
        scratch_shapes=[stat, stat, stat, pltpu.VMEM((ng, HEADS, 2 * tq, 2 * tq), F32),
                        pltpu.VMEM((ng, HEADS, 2 * tq, 2 * tq), BF16), stat,
                        pltpu.VMEM((ng, HEADS, RT_KD, HEAD_W), F32)],
        compiler_params=_cparams("arbitrary", "arbitrary"),
        name="mixer_prompt",
    )(by_seq(q), by_seq(kb), by_seq(vb), bias, by_seq(qr), by_seq(kr), by_seq(vr), by_seq(sg), *tables,
      s0, lamv, gda, grt)
    return od.reshape(b * t, SEG), orr.reshape(b * t, SEG), s_new


CACHE_CHUNK = 1024


def _mixer_sample_kernel(lam_init, q_ref, ck_ref, cv_ref, kn_ref, vn_ref, bnear_ref, bnew_ref,
                         qr_ref, kr_ref, vr_ref, sg_ref, dmat_ref, dq_ref, dk_ref, dc_ref, s0_ref,
                         lamv_ref, gda_ref, grt_ref, od_ref, or_ref, sout_ref, m_scr, l_scr, acc_scr):
    c = pl.program_id(1)
    last = pl.num_programs(1) - 1
    ng, t = q_ref.shape[0], q_ref.shape[1]
    units = [(g, h) for g in range(ng) for h in range(HEADS)]
    qq = [_split_maps(q_ref[g, :, _head(h)]) for g, h in units]

    @pl.when(c == 0)
    def _():
        _flash_init(m_scr, l_scr, acc_scr)

    def cache_keys(bias_ref):
        rows = [pl.ds(h, CACHE_CHUNK, stride=HEADS) for h in range(HEADS)]
        _flash_step_units(
            qq, [ck_ref[g, rows[h], :].astype(BF16) for g, h in units],
            [cv_ref[g, rows[h], :].astype(BF16) for g, h in units],
            [None if bias_ref is None else jnp.concatenate([bias_ref[h, 0]] * 2, axis=0) for g, h in units],
            m_scr, l_scr, acc_scr)

    @pl.when(c < last)
    def _():
        cache_keys(None)

    @pl.when(c == last)
    def _():
        cache_keys(bnear_ref)
        lam = _lam(lamv_ref, lam_init)
        _flash_step_units(
            qq, [kn_ref[g, :, _head(h)] for g, h in units], [vn_ref[g, :, _head(h)] for g, h in units],
            [jnp.concatenate([bnew_ref[h, 0]] * 2, axis=0) for g, h in units], m_scr, l_scr, acc_scr)
        for u, (g, h) in enumerate(units):
            od_ref[g, :, _head(h)] = _flash_finish(t, lam, lam_init, gda_ref, l_scr.at[u], acc_scr.at[u]).astype(BF16)
            o, s_new = _retention_step(qr_ref[g, :, _head(h)], kr_ref[g, :, _head(h)], vr_ref[g, :, _head(h)],
                                       s0_ref[g, h], dmat_ref[h], dq_ref[h], dk_ref[h], dc_ref[h, 0:1])
            sout_ref[g, h] = s_new
            or_ref[g, :, _head(h)] = (_rms(o, grt_ref[...]) * sg_ref[g, :, _head(h)].astype(F32)).astype(BF16)


def _mixer_sample(lam_init, layer, b, t, past, q, cache_k, cache_v, kn, vn, bnear, bnew, qr, kr, vr, sg, s0,
                  lamv, gda, grt):
    assert past % CACHE_CHUNK == 0
    ng = SEQ_GROUP if b % SEQ_GROUP == 0 else 1
    tables = _decay_tables(t)
    by_seq = lambda a: a.reshape(b, t, SEG)
    tile = pl.BlockSpec((ng, t, SEG), lambda bi, c: (bi, 0, 0))
    cache_k, cache_v = (a.reshape(-1, past * HEADS, HEAD_W) for a in (cache_k, cache_v))
    cache = pl.BlockSpec((ng, CACHE_CHUNK * HEADS, HEAD_W), lambda bi, c: (layer * (b // ng) + bi, c, 0))
    state = pl.BlockSpec((ng, HEADS, RT_KD, HEAD_W), lambda bi, c: (bi, 0, 0, 0))
    consts = [bnear, bnew]
    stat = pltpu.VMEM((ng * HEADS, 2 * t, HEAD_W), F32)
    od, orr, s_new = pl.pallas_call(
        functools.partial(_mixer_sample_kernel, lam_init),
        grid=(b // ng, past // CACHE_CHUNK),
        in_specs=[tile, cache, cache, tile, tile] + [_const_spec(a.shape) for a in consts]
                 + [tile, tile, tile, tile] + [_const_spec(a.shape) for a in tables]
                 + [state, _const_spec(lamv.shape), _const_spec(gda.shape), _const_spec(grt.shape)],
        out_specs=[tile, tile, state],
        out_shape=[jax.ShapeDtypeStruct((b, t, SEG), BF16), jax.ShapeDtypeStruct((b, t, SEG), BF16),
                   jax.ShapeDtypeStruct((b, HEADS, RT_KD, HEAD_W), F32)],
        scratch_shapes=[stat, stat, stat],
        compiler_params=_cparams("arbitrary", "arbitrary"),
        name="mixer_sample",
    )(by_seq(q), cache_k, cache_v, by_seq(kn), by_seq(vn), bnear, bnew, by_seq(qr), by_seq(kr), by_seq(vr),
      by_seq(sg), *tables, s0, lamv, gda, grt)
    return od.reshape(b * t, SEG), orr.reshape(b * t, SEG), s_new


FFN_CHUNK = 512


def _channel_kernel(nseq, rows, tiles_per_seq, x_ref, od_ref, or_ref, sd_ref, sr_ref, p_ref, st_ref,
                    wbd_ref, wbr_ref, wo_ref, gffn_ref, wg_ref, wu_ref, cw_ref, cb_ref, wd_ref,
                    gpe_ref, wpe_ref, wpg_ref, y_ref, cn_ref, gbuf):
    i = pl.program_id(0)
    dot = functools.partial(jnp.dot, preferred_element_type=F32)
    mix = (sd_ref[...].astype(F32) * dot(od_ref[...], wbd_ref[...])
           + sr_ref[...].astype(F32) * dot(or_ref[...], wbr_ref[...]))
    h1 = x_ref[...] + dot(mix.astype(BF16), wo_ref[...])

    f = _rms(h1, gffn_ref[...]).astype(BF16)
    pe = dot(p_ref[...].astype(BF16), wpe_ref[...])
    if tiles_per_seq > 1:
        @pl.when(i % tiles_per_seq == 0)
        def _():
            gbuf[6:8, :] = st_ref[0]
    stride = rows + 8
    h2 = h1
    chunks = [slice(c0, min(c0 + FFN_CHUNK, D_FF)) for c0 in range(0, D_FF, FFN_CHUNK)]
    up = lambda cs: (dot(f, wg_ref[:, cs]), dot(f, wu_ref[:, cs]))
    nxt = up(chunks[0])
    for ci, cs in enumerate(chunks):
        g, u = nxt
        if ci + 1 < len(chunks):
            nxt = up(chunks[ci + 1])
        cw = cw_ref[:, cs]
        pieces = []
        for j in range(nseq):
            base = j * stride
            if tiles_per_seq == 1:
                gbuf[base + 6:base + 8, cs] = st_ref[j, :, cs]
            gj = g[j * rows:(j + 1) * rows]
            gbuf[base + 8:base + 8 + rows, cs] = gj
            pieces.append(cb_ref[:, cs] + gbuf[base + 6:base + 6 + rows, cs] * cw[0:1]
                          + gbuf[base + 7:base + 7 + rows, cs] * cw[1:2] + gj * cw[2:3])
            last2 = gbuf[base + rows + 6:base + rows + 8, cs]
            cn_ref[j, :, cs] = last2
            if tiles_per_seq > 1:
                gbuf[6:8, cs] = last2
        gc = pieces[0] if nseq == 1 else jnp.concatenate(pieces, axis=0)
        h2 = h2 + dot((jax.nn.gelu(gc) * u).astype(BF16), wd_ref[cs, :])

    gate = jax.nn.sigmoid(dot(_rms(h2, gpe_ref[...]).astype(BF16), wpg_ref[...]))
    y_ref[...] = h2 + pe * gate


def _channel(b, t, tm, x2, od, orr, sd, sr, p2, st, w):
    n = b * t
    rows = min(tm, t)
    nseq = tm // rows
    tiles_per_seq = t // rows
    row = lambda wd: pl.BlockSpec((tm, wd), lambda i: (i, 0))
    if tiles_per_seq == 1:
        st_spec = pl.BlockSpec((nseq, CONV_W - 1, D_FF), lambda i: (i, 0, 0))
    else:
        st_spec = pl.BlockSpec((1, CONV_W - 1, D_FF), lambda i: (i // tiles_per_seq, 0, 0))
    weights = [w["w_bd"], w["w_br"], w["w_o"], w["g_ffn"], w["w_g"], w["w_u"], w["conv_w"], w["conv_b"],
               w["w_d"], w["g_pe"], w["w_pe"], w["w_pg"]]
    return pl.pallas_call(
        functools.partial(_channel_kernel, nseq, rows, tiles_per_seq),
        grid=(n // tm,),
        in_specs=[row(D_MODEL), row(SEG), row(SEG), row(D_MODEL), row(D_MODEL), row(PE_DIM), st_spec]
                 + [_const_spec(a.shape) for a in weights],
        out_specs=[row(D_MODEL), st_spec],
        out_shape=[jax.ShapeDtypeStruct((n, D_MODEL), F32), jax.ShapeDtypeStruct((b, CONV_W - 1, D_FF), F32)],
        scratch_shapes=[pltpu.VMEM((nseq * (rows + 8), D_FF), F32)],
        compiler_params=_cparams("arbitrary"),
        name="channel",
    )(x2, od, orr, sd, sr, p2, st, *weights)


def _rope_tables(pos):
    half = HEAD_W // 2
    inv = jnp.power(ROPE_BASE, -jnp.arange(half, dtype=F32) / half)
    ang = pos.astype(F32)[:, None] * inv[None, :]
    cos, sin = jnp.cos(ang), jnp.sin(ang)
    return jnp.concatenate([cos, cos], axis=1), jnp.concatenate([-sin, sin], axis=1)


def _prep_weights(i, g_mix, w_in, g_q, g_k, lam_q1, lam_k1, lam_q2, lam_k2, g_da, g_rt, w_bd, w_br, w_o,
                  g_ffn, w_g, w_u, conv_w, conv_b, w_d, g_pe, w_pe, w_pg):
    grp = np.arange(256) // DA_HD
    return dict(
        g_mix=g_mix[i][None], w_in=w_in[i].astype(BF16),
        g_q=jnp.tile(g_q[i], SEG // DA_HD)[None], g_k=jnp.tile(g_k[i], SEG // DA_HD)[None],
        gmat=jnp.asarray((grp[:, None] == grp[None, :]) / DA_HD, BF16),
        lamv=jnp.stack([lam_q1[i], lam_k1[i], lam_q2[i], lam_k2[i]]),
        g_da=g_da[i][None], g_rt=g_rt[i][None],
        w_bd=w_bd[i].astype(BF16), w_br=w_br[i].astype(BF16), w_o=w_o[i].astype(BF16),
        g_ffn=g_ffn[i][None], w_g=w_g[i].astype(BF16), w_u=w_u[i].astype(BF16),
        conv_w=conv_w[i], conv_b=conv_b[i][None], w_d=w_d[i].astype(BF16),
        g_pe=g_pe[i][None], w_pe=w_pe[i].astype(BF16), w_pg=w_pg[i].astype(BF16))


def _layer(h, pe, layer, cache_k, cache_v, s_ret, s_conv, rel_bias, lam_init, w):
    b, t, _ = h.shape
    n = b * t
    past = 0 if cache_k is None else cache_k.shape[2]
    x2 = h.reshape(n, D_MODEL)
    qpos = past + jnp.arange(t, dtype=jnp.int32)

    tm_in = min(512, n)
    cos_t, sin_t = _rope_tables(qpos)
    if t < tm_in:
        cos_t, sin_t = jnp.tile(cos_t, (tm_in // t, 1)), jnp.tile(sin_t, (tm_in // t, 1))
    q, k32, kb, v32, vb, qr, kr, vr, sg, sd, sr = _inproj(
        x2, cos_t, sin_t, w["g_mix"], w["w_in"], w["g_q"], w["g_k"], w["gmat"], tm_in)

    if cache_k is None:
        tq = ATTN_BLOCK
        assert t % tq == 0 and tq % CHUNK == 0 and tq >= MAX_DIST
        bias = _bias_tiles(rel_bias, tq + jnp.arange(tq, dtype=jnp.int32)[None],
                           jnp.arange(2 * tq, dtype=jnp.int32)[None])
        s0 = jnp.zeros((b, HEADS, RT_KD, HEAD_W), F32)
        od, orr, s_new = _mixer_prompt(lam_init, b, t, q, kb, vb, bias, qr, kr, vr, sg, s0,
                                       w["lamv"], w["g_da"], w["g_rt"])
        st = jnp.zeros((b, CONV_W - 1, D_FF), F32)
    else:
        assert past % CHUNK == 0 and t <= CHUNK and CACHE_CHUNK >= MAX_DIST
        near = jnp.arange(past - CACHE_CHUNK, past, dtype=jnp.int32)
        bnear = _bias_tiles(rel_bias, qpos[None], near[None])
        bnew = _bias_tiles(rel_bias, qpos[None], qpos[None])
        od, orr, s_new = _mixer_sample(lam_init, layer, b, t, past, q, cache_k, cache_v, kb, vb, bnear, bnew,
                                       qr, kr, vr, sg, s_ret, w["lamv"], w["g_da"], w["g_rt"])
        st = s_conv

    tm_ch = min(256, n)
    y, conv_new = _channel(b, t, tm_ch, x2, od, orr, sd, sr, pe.reshape(n, PE_DIM), st, w)
    return (y.reshape(b, t, D_MODEL), k32.reshape(b, t, HEADS, HEAD_W), v32.reshape(b, t, HEADS, HEAD_W),
            s_new, conv_new)


def kernel(x_prompt, x_sample, p_prompt, p_sample, cache_k, cache_v, state_ret, state_conv, rel_bias, g_mix, w_in, g_q, g_k, lam_q1, lam_k1, lam_q2, lam_k2, g_da, g_rt, w_bd, w_br, w_o, g_ffn, w_g, w_u, conv_w, conv_b, w_d, g_pe, w_pe, w_pg):
    depth = w_in.shape[0]
    hp, hs = x_prompt, x_sample
    outs = [[] for _ in range(8)]
    for i in range(depth):
        lam_init = 0.8 - 0.6 * math.exp(-0.3 * i)
        w = _prep_weights(i, g_mix, w_in, g_q, g_k, lam_q1, lam_k1, lam_q2, lam_k2, g_da, g_rt, w_bd, w_br,
                          w_o, g_ffn, w_g, w_u, conv_w, conv_b, w_d, g_pe, w_pe, w_pg)
        hp, kp, vp, rp, cp = _layer(hp, p_prompt[i], i, None, None, None, None, rel_bias, lam_init, w)
        hs, ks, vs, rs, cs = _layer(hs, p_sample[i], i, cache_k, cache_v, state_ret[i], state_conv[i],
                                    rel_bias, lam_init, w)
        for lst, val in zip(outs, (kp, vp, rp, cp, ks, vs, rs, cs)):
            lst.append(val)
    return (hp, hs) + tuple(jnp.stack(o) for o in outs)
```

```python
import functools
import math

import jax
import jax.numpy as jnp
from jax import lax
from jax.experimental import pallas as pl
from jax.experimental.pallas import tpu as pltpu

D_MODEL = 1024
CHUNK = 64
PE_DIM = 256
HEADS = 4
HEAD_W = 128
DA_HD = 64
RT_KD = 128
D_FF = 2816
CONV_W = 3
N_BUCKETS = 32
MAX_DIST = 128
EPS = 1e-6
ROPE_BASE = 10000.0
SEG = HEADS * HEAD_W
D_IN = 7 * SEG + 2 * D_MODEL

MASK_VALUE = -1e30
V7X_VMEM_LIMIT = 56 * 1024 * 1024
ATTN_BLOCK = 256
F32 = jnp.float32
BF16 = jnp.bfloat16


def _cparams(*sem):
    return pltpu.CompilerParams(dimension_semantics=sem, vmem_limit_bytes=V7X_VMEM_LIMIT)


def _const_spec(shape):
    nd = len(shape)
    return pl.BlockSpec(shape, lambda *_: (0,) * nd, pipeline_mode=pl.Buffered(1))


def _rms(x, g):
    return x * lax.rsqrt(jnp.mean(x * x, axis=-1, keepdims=True) + EPS) * g


def _t5_bucket(rel):
    nb = N_BUCKETS // 2
    ret = jnp.where(rel > 0, nb, 0)
    n = jnp.abs(rel)
    max_exact = nb // 2
    nf = jnp.maximum(n, 1).astype(F32)
    large = max_exact + (jnp.log(nf / max_exact) / math.log(MAX_DIST / max_exact) * (nb - max_exact)).astype(jnp.int32)
    large = jnp.minimum(large, nb - 1)
    return ret + jnp.where(n < max_exact, n, large)


FAR_BUCKET = N_BUCKETS // 2 - 1
LOG2E = math.log2(math.e)


def _bias_kernel(idx_ref, madd_ref, rb_ref, out_ref):
    h = pl.program_id(0)
    idx = idx_ref[...]
    bias = jnp.zeros(idx.shape, F32)
    for b in range(N_BUCKETS):
        bias = jnp.where(idx == b, rb_ref[b, h], bias)
    out_ref[0] = (bias - rb_ref[FAR_BUCKET, h]) * LOG2E + madd_ref[...]


def _bias_tiles(rel_bias, qpos, kpos):
    rel = kpos[:, None, :] - qpos[:, :, None]
    idx = _t5_bucket(rel).astype(jnp.int32)
    madd = jnp.where((kpos[:, None, :] // CHUNK) <= (qpos[:, :, None] // CHUNK), 0.0, MASK_VALUE).astype(F32)
    n, r, c = idx.shape
    return pl.pallas_call(
        _bias_kernel,
        grid=(HEADS,),
        in_specs=[pl.BlockSpec((n, r, c), lambda h: (0, 0, 0)),
                  pl.BlockSpec((n, r, c), lambda h: (0, 0, 0)),
                  pl.BlockSpec(memory_space=pltpu.SMEM)],
        out_specs=pl.BlockSpec((1, n, r, c), lambda h: (h, 0, 0, 0)),
        out_shape=jax.ShapeDtypeStruct((HEADS, n, r, c), F32),
        compiler_params=_cparams("arbitrary"),
        name="bias_tiles",
    )(idx, madd, rel_bias)


def _inproj_kernel(x_ref, cos_ref, sin_ref, gmix_ref, w_ref, gq_ref, gk_ref,
                   q_ref, k32_ref, kb_ref, v32_ref, vb_ref, qr_ref, kr_ref, vr_ref, sg_ref, sd_ref, sr_ref):
    a = _rms(x_ref[...], gmix_ref[...]).astype(BF16)

    def seg(c0, c1):
        return jnp.dot(a, w_ref[:, c0:c1], preferred_element_type=F32)

    def group_norm(z, g):
        outs = []
        for h in range(HEADS):
            zs = z[:, h * HEAD_W:(h + 1) * HEAD_W]
            sq = zs * zs
            low = lax.broadcasted_iota(jnp.int32, sq.shape, 1) < DA_HD
            s_all = jnp.sum(sq, axis=1, keepdims=True)
            s_low = jnp.sum(jnp.where(low, sq, 0.0), axis=1, keepdims=True)
            ms = jnp.where(low, s_low, s_all - s_low) * (1.0 / DA_HD)
            outs.append(zs * lax.rsqrt(ms + EPS))
        return jnp.concatenate(outs, axis=1) * g

    def rotary(z):
        cos, sin = cos_ref[...], sin_ref[...]
        outs = []
        for h in range(HEADS):
            zs = z[:, h * HEAD_W:(h + 1) * HEAD_W]
            outs.append(zs * cos + pltpu.roll(zs, HEAD_W // 2, 1) * sin)
        return jnp.concatenate(outs, axis=1)

    def by_head(z, out32_ref):
        for h in range(HEADS):
            out32_ref[pl.ds(h, z.shape[0], stride=HEADS), :] = z[:, h * HEAD_W:(h + 1) * HEAD_W]

    def put_q(z):
        q_ref[...] = (group_norm(z, gq_ref[...]) * (DA_HD ** -0.5 * LOG2E)).astype(BF16)

    def put_k(z):
        k = group_norm(z, gk_ref[...])
        kb_ref[...] = k.astype(BF16)
        by_head(k, k32_ref)

    def put_v(z):
        vb_ref[...] = z.astype(BF16)
        by_head(z, v32_ref)

    def put_qr(z):
        qr_ref[...] = rotary(z).astype(BF16)

    def put_kr(z):
        kr_ref[...] = (rotary(z) * RT_KD ** -0.5).astype(BF16)

    def put_vr(z):
        vr_ref[...] = z.astype(BF16)

    def put_sg(z):
        sg_ref[...] = (z * jax.nn.sigmoid(z)).astype(BF16)

    def put_sd(z):
        sd_ref[...] = jax.nn.sigmoid(z).astype(BF16)

    def put_sr(z):
        sr_ref[...] = jax.nn.sigmoid(z).astype(BF16)

    puts = [put_q, put_k, put_v, put_qr, put_kr, put_vr, put_sg, put_sd, put_sr]
    cols = [0, SEG, 2 * SEG, 3 * SEG, 4 * SEG, 5 * SEG, 6 * SEG, 7 * SEG, 7 * SEG + D_MODEL, D_IN]
    z_next = seg(cols[0], cols[1])
    for i, put in enumerate(puts):
        z = z_next
        if i + 1 < len(puts):
            z_next = seg(cols[i + 1], cols[i + 2])
        put(z)


def _inproj(x2, cos_t, sin_t, gmix, w_in, gq, gk, tm):
    n = x2.shape[0]
    nt = cos_t.shape[0] // tm
    row = lambda w: pl.BlockSpec((tm, w), lambda i: (i, 0))
    tab = pl.BlockSpec((tm, HEAD_W), lambda i: (i % nt, 0))
    out_w = [(SEG, BF16), (None, F32), (SEG, BF16), (None, F32), (SEG, BF16), (SEG, BF16), (SEG, BF16),
             (SEG, BF16), (SEG, BF16), (D_MODEL, BF16), (D_MODEL, BF16)]
    by_head = pl.BlockSpec((tm * HEADS, HEAD_W), lambda i: (i, 0))
    return pl.pallas_call(
        _inproj_kernel,
        grid=(n // tm,),
        in_specs=[row(D_MODEL), tab, tab, _const_spec((1, D_MODEL)), _const_spec((D_MODEL, D_IN)),
                  _const_spec((1, SEG)), _const_spec((1, SEG))],
        out_specs=[by_head if w is None else row(w) for w, _ in out_w],
        out_shape=[jax.ShapeDtypeStruct((n * HEADS, HEAD_W) if w is None else (n, w), dt) for w, dt in out_w],
        compiler_params=_cparams("arbitrary"),
        name="inproj",
    )(x2, cos_t, sin_t, gmix, w_in, gq, gk)


def _split_maps(q):
    lane = lax.broadcasted_iota(jnp.int32, q.shape, 1)
    zero = jnp.zeros_like(q)
    return jnp.concatenate([jnp.where(lane < DA_HD, q, zero), jnp.where(lane >= DA_HD, q, zero)], axis=0)


def _flash_init(m_scr, l_scr, acc_scr):
    m_scr[...] = jnp.full(m_scr.shape, MASK_VALUE, F32)
    l_scr[...] = jnp.zeros(l_scr.shape, F32)
    acc_scr[...] = jnp.zeros(acc_scr.shape, F32)


def _flash_step(qq, k, v, bias, m_scr, l_scr, acc_scr):
    s = lax.dot_general(qq, k, (((1,), (1,)), ((), ())), preferred_element_type=F32)
    if bias is not None:
        s = s + bias
    m_prev = m_scr[...]
    m_new = jnp.maximum(m_prev, jnp.max(s, axis=1, keepdims=True))
    alpha = jnp.exp2(m_prev - m_new)
    p = jnp.exp2(s - m_new[:, :1])
    l_scr[...] = alpha * l_scr[...] + jnp.sum(p, axis=1, keepdims=True)
    acc_scr[...] = alpha * acc_scr[...] + jnp.dot(p.astype(BF16), v, preferred_element_type=F32)
    m_scr[...] = m_new


def _flash_step_units(qqs, ks, vs, biases, m_scr, l_scr, acc_scr):
    nt = (((1,), (1,)), ((), ()))
    units = range(len(qqs))
    ss = [lax.dot_general(qq, k, nt, preferred_element_type=F32) for qq, k in zip(qqs, ks)]
    ss = [s if b is None else s + b for s, b in zip(ss, biases)]
    m_prev = [m_scr[u] for u in units]
    l_prev = [l_scr[u] for u in units]
    acc_prev = [acc_scr[u] for u in units]
    m_new = [jnp.maximum(mp, jnp.max(s, axis=1, keepdims=True)) for mp, s in zip(m_prev, ss)]
    alpha = [jnp.exp2(mp - mn) for mp, mn in zip(m_prev, m_new)]
    ps = [jnp.exp2(s - mn[:, :1]) for s, mn in zip(ss, m_new)]
    pv = [jnp.dot(p.astype(BF16), v, preferred_element_type=F32) for p, v in zip(ps, vs)]
    for u in units:
        l_scr[u] = alpha[u] * l_prev[u] + jnp.sum(ps[u], axis=1, keepdims=True)
        acc_scr[u] = alpha[u] * acc_prev[u] + pv[u]
        m_scr[u] = m_new[u]


FLASH_STRIP = 64


def _flash_step_staged(qqs, ks, vs, bias_fn, first, m_scr, l_scr, acc_scr, s_scr, p_scr, a_scr):
    rows = s_scr.shape[1]
    tk = ks[0].shape[0]
    for h in range(HEADS):
        s_scr[h, :, :tk] = lax.dot_general(qqs[h], ks[h], (((1,), (1,)), ((), ())), preferred_element_type=F32)
        for r0 in range(0, rows, FLASH_STRIP):
            rs = slice(r0, r0 + FLASH_STRIP)
            s = s_scr[h, rs, :tk]
            if bias_fn is not None:
                s = s + bias_fn(h, r0, FLASH_STRIP)
            m_new = jnp.max(s, axis=1, keepdims=True)
            if first:
                m_new = jnp.broadcast_to(m_new, (FLASH_STRIP, HEAD_W))
            else:
                m_prev = m_scr[h, rs, :]
                m_new = jnp.maximum(m_prev, m_new)
                a_scr[h, rs, :] = jnp.exp2(m_prev - m_new)
            m_scr[h, rs, :] = m_new
            p_scr[h, rs, :tk] = jnp.exp2(s - jnp.concatenate([m_new] * (tk // HEAD_W), axis=1)).astype(BF16)
        v = vs[h]
        pv = jnp.dot(p_scr[h, :, :tk], jnp.concatenate([v, jnp.ones_like(v)], axis=1),
                     preferred_element_type=F32)
        if first:
            acc_scr[h] = pv[:, :HEAD_W]
            l_scr[h] = pv[:, HEAD_W:]
        else:
            alpha = a_scr[h]
            acc_scr[h] = alpha * acc_scr[h] + pv[:, :HEAD_W]
            l_scr[h] = alpha * l_scr[h] + pv[:, HEAD_W:]


def _lam(lamv_ref, lam_init):
    lv = lamv_ref[...]
    s1 = jnp.sum(lv[0:1] * lv[1:2], axis=1, keepdims=True)
    s2 = jnp.sum(lv[2:3] * lv[3:4], axis=1, keepdims=True)
    return jnp.exp(s1) - jnp.exp(s2) + lam_init


def _flash_finish(t, lam, lam_init, gda_ref, l_scr, acc_scr):
    o = acc_scr[...] / l_scr[...]
    o = o[:t] - lam * o[t:]
    return _rms(o, gda_ref[...]) * (1.0 - lam_init)


def _retention_step(qr, kr, vr, s_prev, dmat, dq, dk, dc):
    att = lax.dot_general(qr, kr, (((1,), (1,)), ((), ())), preferred_element_type=F32) * dmat
    o = jnp.dot(att.astype(BF16), vr, preferred_element_type=F32)
    o = o + jnp.dot(qr, s_prev.astype(BF16), preferred_element_type=F32) * dq
    kdec = (kr.astype(F32) * dk).astype(BF16)
    s_new = s_prev * dc + lax.dot_general(kdec, vr, (((0,), (0,)), ((), ())), preferred_element_type=F32)
    return o, s_new


def _decay_tables(c):
    log_g = jnp.log1p(-jnp.power(2.0, -5.0 - jnp.arange(HEADS, dtype=F32)))
    idx = jnp.arange(c, dtype=F32)
    diff = idx[:, None] - idx[None, :]
    dmat = jnp.where(diff >= 0, jnp.exp(log_g[:, None, None] * jnp.maximum(diff, 0.0)), 0.0)
    dq = jnp.broadcast_to(jnp.exp(log_g[:, None] * (idx + 1.0)[None, :])[:, :, None], (HEADS, c, HEAD_W))
    dk = jnp.broadcast_to(jnp.exp(log_g[:, None] * (c - 1.0 - idx)[None, :])[:, :, None], (HEADS, c, HEAD_W))
    dc = jnp.broadcast_to(jnp.exp(log_g * c)[:, None, None], (HEADS, 8, HEAD_W))
    return dmat.astype(F32), dq.astype(F32), dk.astype(F32), dc.astype(F32)


def _head(h):
    return slice(h * HEAD_W, (h + 1) * HEAD_W)


SEQ_GROUP = 2


def _mixer_prompt_kernel(lam_init, q_ref, k_ref, v_ref, bias_ref, qr_ref, kr_ref, vr_ref, sg_ref,
                         dmat_ref, dq_ref, dk_ref, dc_ref, s0_ref, lamv_ref, gda_ref, grt_ref,
                         od_ref, or_ref, sout_ref, m_scr, l_scr, acc_scr, sc_scr, p_scr, a_scr, s_scr):
    qb = pl.program_id(1)
    ng, t = q_ref.shape[0], q_ref.shape[1]
    qq = [[_split_maps(q_ref[g, :, _head(h)]) for h in range(HEADS)] for g in range(ng)]

    def key_step(kb, nblk, bias_cols, first=False):
        off = pl.multiple_of(kb * t, t)
        bias_fn = None if bias_cols is None else (lambda h, r0, n: bias_ref[h, 0, r0 % t:r0 % t + n, bias_cols])
        for g in range(ng):
            _flash_step_staged(qq[g], [k_ref[g, pl.ds(off, nblk * t), _head(h)] for h in range(HEADS)],
                               [v_ref[g, pl.ds(off, nblk * t), _head(h)] for h in range(HEADS)], bias_fn, first,
                               m_scr.at[g], l_scr.at[g], acc_scr.at[g], sc_scr.at[g], p_scr.at[g], a_scr.at[g])

    key_step(qb, 1, slice(t, 2 * t), first=True)

    @pl.when(qb >= 1)
    def _():
        key_step(qb - 1, 1, slice(0, t))

    n_far = jnp.maximum(qb - 1, 0)

    @pl.when(n_far % 2 == 1)
    def _():
        key_step(n_far - 1, 1, None)

    def far_body(i, carry):
        key_step(2 * i, 2, None)
        return carry

    lax.fori_loop(0, n_far // 2, far_body, 0)
    lam = _lam(lamv_ref, lam_init)

    @pl.when(qb == 0)
    def _():
        s_scr[...] = s0_ref[...]


    for g in range(ng):
        for h in range(HEADS):
            od_ref[g, :, _head(h)] = _flash_finish(
                t, lam, lam_init, gda_ref, l_scr.at[g, h], acc_scr.at[g, h]).astype(BF16)
            o, s_new = _retention_step(qr_ref[g, :, _head(h)], kr_ref[g, :, _head(h)], vr_ref[g, :, _head(h)],
                                       s_scr[g, h], dmat_ref[h], dq_ref[h], dk_ref[h], dc_ref[h, 0:1])
            s_scr[g, h] = s_new
            sout_ref[g, h] = s_new
            or_ref[g, :, _head(h)] = (_rms(o, grt_ref[...]) * sg_ref[g, :, _head(h)].astype(F32)).astype(BF16)


def _mixer_prompt(lam_init, b, t, q, kb, vb, bias, qr, kr, vr, sg, s0, lamv, gda, grt):
    tq = ATTN_BLOCK
    nq = t // tq
    ng = SEQ_GROUP if b % SEQ_GROUP == 0 else 1
    tables = _decay_tables(tq)
    by_seq = lambda a: a.reshape(b, t, SEG)
    tile = pl.BlockSpec((ng, tq, SEG), lambda bi, qb: (bi, qb, 0))
    seq = pl.BlockSpec((ng, t, SEG), lambda bi, qb: (bi, 0, 0))
    state = pl.BlockSpec((ng, HEADS, RT_KD, HEAD_W), lambda bi, qb: (bi, 0, 0, 0))
    stat = pltpu.VMEM((ng, HEADS, 2 * tq, HEAD_W), F32)
    od, orr, s_new = pl.pallas_call(
        functools.partial(_mixer_prompt_kernel, lam_init),
        grid=(b // ng, nq),
        in_specs=[tile, seq, seq, _const_spec(bias.shape), tile, tile, tile, tile]
                 + [_const_spec(a.shape) for a in tables]
                 + [state, _const_spec(lamv.shape), _const_spec(gda.shape), _const_spec(grt.shape)],
        out_specs=[tile, tile, state],
        out_shape=[jax.ShapeDtypeStruct((b, t, SEG), BF16), jax.ShapeDtypeStruct((b, t, SEG), BF16),
                   jax.ShapeDtypeStruct((b, HEADS, RT_KD, HEAD_W), F32)],
        scratch_shapes=[stat, stat, stat, pltpu.VMEM((ng, HEADS, 2 * tq, 2 * tq), F32),
                        pltpu.VMEM((ng, HEADS, 2 * tq, 2 * tq), BF16), stat,
                        pltpu.VMEM((ng, HEADS, RT_KD, HEAD_W), F32)],
        compiler_params=_cparams("arbitrary", "arbitrary"),
        name="mixer_prompt",
    )(by_seq(q), by_seq(kb), by_seq(vb), bias, by_seq(qr), by_seq(kr), by_seq(vr), by_seq(sg), *tables,
      s0, lamv, gda, grt)
    return od.reshape(b * t, SEG), orr.reshape(b * t, SEG), s_new


CACHE_CHUNK = 1024


def _mixer_sample_kernel(lam_init, q_ref, ck_ref, cv_ref, kn_ref, vn_ref, bnear_ref, bnew_ref,
                         qr_ref, kr_ref, vr_ref, sg_ref, dmat_ref, dq_ref, dk_ref, dc_ref, s0_ref,
                         lamv_ref, gda_ref, grt_ref, od_ref, or_ref, sout_ref, m_scr, l_scr, acc_scr):
    c = pl.program_id(1)
    last = pl.num_programs(1) - 1
    ng, t = q_ref.shape[0], q_ref.shape[1]
    units = [(g, h) for g in range(ng) for h in range(HEADS)]
    qq = [_split_maps(q_ref[g, :, _head(h)]) for g, h in units]

    @pl.when(c == 0)
    def _():
        _flash_init(m_scr, l_scr, acc_scr)

    def cache_keys(bias_ref):
        rows = [pl.ds(h, CACHE_CHUNK, stride=HEADS) for h in range(HEADS)]
        _flash_step_units(
            qq, [ck_ref[g, rows[h], :].astype(BF16) for g, h in units],
            [cv_ref[g, rows[h], :].astype(BF16) for g, h in units],
            [None if bias_ref is None else jnp.concatenate([bias_ref[h, 0]] * 2, axis=0) for g, h in units],
            m_scr, l_scr, acc_scr)

    @pl.when(c < last)
    def _():
        cache_keys(None)

    @pl.when(c == last)
    def _():
        cache_keys(bnear_ref)
        lam = _lam(lamv_ref, lam_init)
        _flash_step_units(
            qq, [kn_ref[g, :, _head(h)] for g, h in units], [vn_ref[g, :, _head(h)] for g, h in units],
            [jnp.concatenate([bnew_ref[h, 0]] * 2, axis=0) for g, h in units], m_scr, l_scr, acc_scr)
        for u, (g, h) in enumerate(units):
            od_ref[g, :, _head(h)] = _flash_finish(t, lam, lam_init, gda_ref, l_scr.at[u], acc_scr.at[u]).astype(BF16)
            o, s_new = _retention_step(qr_ref[g, :, _head(h)], kr_ref[g, :, _head(h)], vr_ref[g, :, _head(h)],
                                       s0_ref[g, h], dmat_ref[h], dq_ref[h], dk_ref[h], dc_ref[h, 0:1])
            sout_ref[g, h] = s_new
            or_ref[g, :, _head(h)] = (_rms(o, grt_ref[...]) * sg_ref[g, :, _head(h)].astype(F32)).astype(BF16)


def _mixer_sample(lam_init, layer, b, t, past, q, cache_k, cache_v, kn, vn, bnear, bnew, qr, kr, vr, sg, s0,
                  lamv, gda, grt):
    assert past % CACHE_CHUNK == 0
    ng = SEQ_GROUP if b % SEQ_GROUP == 0 else 1
    tables = _decay_tables(t)
    by_seq = lambda a: a.reshape(b, t, SEG)
    tile = pl.BlockSpec((ng, t, SEG), lambda bi, c: (bi, 0, 0))
    cache_k, cache_v = (a.reshape(-1, past * HEADS, HEAD_W) for a in (cache_k, cache_v))
    cache = pl.BlockSpec((ng, CACHE_CHUNK * HEADS, HEAD_W), lambda bi, c: (layer * (b // ng) + bi, c, 0))
    state = pl.BlockSpec((ng, HEADS, RT_KD, HEAD_W), lambda bi, c: (bi, 0, 0, 0))
    consts = [bnear, bnew]
    stat = pltpu.VMEM((ng * HEADS, 2 * t, HEAD_W), F32)
    od, orr, s_new = pl.pallas_call(
        functools.partial(_mixer_sample_kernel, lam_init),
        grid=(b // ng, past // CACHE_CHUNK),
        in_specs=[tile, cache, cache, tile, tile] + [_const_spec(a.shape) for a in consts]
                 + [tile, tile, tile, tile] + [_const_spec(a.shape) for a in tables]
                 + [state, _const_spec(lamv.shape), _const_spec(gda.shape), _const_spec(grt.shape)],
        out_specs=[tile, tile, state],
        out_shape=[jax.ShapeDtypeStruct((b, t, SEG), BF16), jax.ShapeDtypeStruct((b, t, SEG), BF16),
                   jax.ShapeDtypeStruct((b, HEADS, RT_KD, HEAD_W), F32)],
        scratch_shapes=[stat, stat, stat],
        compiler_params=_cparams("arbitrary", "arbitrary"),
        name="mixer_sample",
    )(by_seq(q), cache_k, cache_v, by_seq(kn), by_seq(vn), bnear, bnew, by_seq(qr), by_seq(kr), by_seq(vr),
      by_seq(sg), *tables, s0, lamv, gda, grt)
    return od.reshape(b * t, SEG), orr.reshape(b * t, SEG), s_new


FFN_CHUNK = 512


def _channel_kernel(nseq, rows, tiles_per_seq, x_ref, od_ref, or_ref, sd_ref, sr_ref, p_ref, st_ref,
                    wbd_ref, wbr_ref, wo_ref, gffn_ref, wg_ref, wu_ref, cw_ref, cb_ref, wd_ref,
                    gpe_ref, wpe_ref, wpg_ref, y_ref, cn_ref, gbuf):
    i = pl.program_id(0)
    dot = functools.partial(jnp.dot, preferred_element_type=F32)
    mix = (sd_ref[...].astype(F32) * dot(od_ref[...], wbd_ref[...])
           + sr_ref[...].astype(F32) * dot(or_ref[...], wbr_ref[...]))
    h1 = x_ref[...] + dot(mix.astype(BF16), wo_ref[...])

    f = _rms(h1, gffn_ref[...]).astype(BF16)
    pe = dot(p_ref[...].astype(BF16), wpe_ref[...])
    if tiles_per_seq > 1:
        @pl.when(i % tiles_per_seq == 0)
        def _():
            gbuf[6:8, :] = st_ref[0]
    stride = rows + 8
    h2 = h1
    chunks = [slice(c0, min(c0 + FFN_CHUNK, D_FF)) for c0 in range(0, D_FF, FFN_CHUNK)]
    up = lambda cs: (dot(f, wg_ref[:, cs]), dot(f, wu_ref[:, cs]))
    nxt = up(chunks[0])
    for ci, cs in enumerate(chunks):
        g, u = nxt
        if ci + 1 < len(chunks):
            nxt = up(chunks[ci + 1])
        cw = cw_ref[:, cs]
        pieces = []
        for j in range(nseq):
            base = j * stride
            if tiles_per_seq == 1:
                gbuf[base + 6:base + 8, cs] = st_ref[j, :, cs]
            gj = g[j * rows:(j + 1) * rows]
            gbuf[base + 8:base + 8 + rows, cs] = gj
            pieces.append(cb_ref[:, cs] + gbuf[base + 6:base + 6 + rows, cs] * cw[0:1]
                          + gbuf[base + 7:base + 7 + rows, cs] * cw[1:2] + gj * cw[2:3])
            last2 = gbuf[base + rows + 6:base + rows + 8, cs]
            cn_ref[j, :, cs] = last2
            if tiles_per_seq > 1:
                gbuf[6:8, cs] = last2
        gc = pieces[0] if nseq == 1 else jnp.concatenate(pieces, axis=0)
        h2 = h2 + dot((jax.nn.gelu(gc) * u).astype(BF16), wd_ref[cs, :])

    gate = jax.nn.sigmoid(dot(_rms(h2, gpe_ref[...]).astype(BF16), wpg_ref[...]))
    y_ref[...] = h2 + pe * gate


def _channel(b, t, tm, x2, od, orr, sd, sr, p2, st, w):
    n = b * t
    rows = min(tm, t)
    nseq = tm // rows
    tiles_per_seq = t // rows
    row = lambda wd: pl.BlockSpec((tm, wd), lambda i: (i, 0))
    if tiles_per_seq == 1:
        st_spec = pl.BlockSpec((nseq, CONV_W - 1, D_FF), lambda i: (i, 0, 0))
    else:
        st_spec = pl.BlockSpec((1, CONV_W - 1, D_FF), lambda i: (i // tiles_per_seq, 0, 0))
    weights = [w["w_bd"], w["w_br"], w["w_o"], w["g_ffn"], w["w_g"], w["w_u"], w["conv_w"], w["conv_b"],
               w["w_d"], w["g_pe"], w["w_pe"], w["w_pg"]]
    return pl.pallas_call(
        functools.partial(_channel_kernel, nseq, rows, tiles_per_seq),
        grid=(n // tm,),
        in_specs=[row(D_MODEL), row(SEG), row(SEG), row(D_MODEL), row(D_MODEL), row(PE_DIM), st_spec]
                 + [_const_spec(a.shape) for a in weights],
        out_specs=[row(D_MODEL), st_spec],
        out_shape=[jax.ShapeDtypeStruct((n, D_MODEL), F32), jax.ShapeDtypeStruct((b, CONV_W - 1, D_FF), F32)],
        scratch_shapes=[pltpu.VMEM((nseq * (rows + 8), D_FF), F32)],
        compiler_params=_cparams("arbitrary"),
        name="channel",
    )(x2, od, orr, sd, sr, p2, st, *weights)


def _rope_tables(pos):
    half = HEAD_W // 2
    inv = jnp.power(ROPE_BASE, -jnp.arange(half, dtype=F32) / half)
    ang = pos.astype(F32)[:, None] * inv[None, :]
    cos, sin = jnp.cos(ang), jnp.sin(ang)
    return jnp.concatenate([cos, cos], axis=1), jnp.concatenate([-sin, sin], axis=1)


def _prep_weights(i, g_mix, w_in, g_q, g_k, lam_q1, lam_k1, lam_q2, lam_k2, g_da, g_rt, w_bd, w_br, w_o,
                  g_ffn, w_g, w_u, conv_w, conv_b, w_d, g_pe, w_pe, w_pg):
    return dict(
        g_mix=g_mix[i][None], w_in=w_in[i].astype(BF16),
        g_q=jnp.tile(g_q[i], SEG // DA_HD)[None], g_k=jnp.tile(g_k[i], SEG // DA_HD)[None],
        lamv=jnp.stack([lam_q1[i], lam_k1[i], lam_q2[i], lam_k2[i]]),
        g_da=g_da[i][None], g_rt=g_rt[i][None],
        w_bd=w_bd[i].astype(BF16), w_br=w_br[i].astype(BF16), w_o=w_o[i].astype(BF16),
        g_ffn=g_ffn[i][None], w_g=w_g[i].astype(BF16), w_u=w_u[i].astype(BF16),
        conv_w=conv_w[i], conv_b=conv_b[i][None], w_d=w_d[i].astype(BF16),
        g_pe=g_pe[i][None], w_pe=w_pe[i].astype(BF16), w_pg=w_pg[i].astype(BF16))


def _layer(h, pe, layer, cache_k, cache_v, s_ret, s_conv, rel_bias, lam_init, w):
    b, t, _ = h.shape
    n = b * t
    past = 0 if cache_k is None else cache_k.shape[2]
    x2 = h.reshape(n, D_MODEL)
    qpos = past + jnp.arange(t, dtype=jnp.int32)

    tm_in = min(512, n)
    cos_t, sin_t = _rope_tables(qpos)
    if t < tm_in:
        cos_t, sin_t = jnp.tile(cos_t, (tm_in // t, 1)), jnp.tile(sin_t, (tm_in // t, 1))
    q, k32, kb, v32, vb, qr, kr, vr, sg, sd, sr = _inproj(
        x2, cos_t, sin_t, w["g_mix"], w["w_in"], w["g_q"], w["g_k"], tm_in)

    if cache_k is None:
        tq = ATTN_BLOCK
        assert t % tq == 0 and tq % CHUNK == 0 and tq >= MAX_DIST
        bias = _bias_tiles(rel_bias, tq + jnp.arange(tq, dtype=jnp.int32)[None],
                           jnp.arange(2 * tq, dtype=jnp.int32)[None])
        s0 = jnp.zeros((b, HEADS, RT_KD, HEAD_W), F32)
        od, orr, s_new = _mixer_prompt(lam_init, b, t, q, kb, vb, bias, qr, kr, vr, sg, s0,
                                       w["lamv"], w["g_da"], w["g_rt"])
        st = jnp.zeros((b, CONV_W - 1, D_FF), F32)
    else:
        assert past % CHUNK == 0 and t <= CHUNK and CACHE_CHUNK >= MAX_DIST
        near = jnp.arange(past - CACHE_CHUNK, past, dtype=jnp.int32)
        bnear = _bias_tiles(rel_bias, qpos[None], near[None])
        bnew = _bias_tiles(rel_bias, qpos[None], qpos[None])
        od, orr, s_new = _mixer_sample(lam_init, layer, b, t, past, q, cache_k, cache_v, kb, vb, bnear, bnew,
                                       qr, kr, vr, sg, s_ret, w["lamv"], w["g_da"], w["g_rt"])
        st = s_conv

    tm_ch = min(512, n)
    y, conv_new = _channel(b, t, tm_ch, x2, od, orr, sd, sr, pe.reshape(n, PE_DIM), st, w)
    return (y.reshape(b, t, D_MODEL), k32.reshape(b, t, HEADS, HEAD_W), v32.reshape(b, t, HEADS, HEAD_W),
            s_new, conv_new)


def kernel(x_prompt, x_sample, p_prompt, p_sample, cache_k, cache_v, state_ret, state_conv, rel_bias, g_mix, w_in, g_q, g_k, lam_q1, lam_k1, lam_q2, lam_k2, g_da, g_rt, w_bd, w_br, w_o, g_ffn, w_g, w_u, conv_w, conv_b, w_d, g_pe, w_pe, w_pg):
    depth = w_in.shape[0]
    hp, hs = x_prompt, x_sample
    outs = [[] for _ in range(8)]
    for i in range(depth):
        lam_init = 0.8 - 0.6 * math.exp(-0.3 * i)
        w = _prep_weights(i, g_mix, w_in, g_q, g_k, lam_q1, lam_k1, lam_q2, lam_k2, g_da, g_rt, w_bd, w_br,
                          w_o, g_ffn, w_g, w_u, conv_w, conv_b, w_d, g_pe, w_pe, w_pg)
        hp, kp, vp, rp, cp = _layer(hp, p_prompt[i], i, None, None, None, None, rel_bias, lam_init, w)
        hs, ks, vs, rs, cs = _layer(hs, p_sample[i], i, cache_k, cache_v, state_ret[i], state_conv[i],
                                    rel_bias, lam_init, w)
        for lst, val in zip(outs, (kp, vp, rp, cp, ks, vs, rs, cs)):
            lst.append(val)
    return (hp, hs) + tuple(jnp.stack(o) for o in outs)
```

```python
import functools
import math

import jax
import jax.numpy as jnp
from jax import lax
from jax.experimental import pallas as pl
from jax.experimental.pallas import tpu as pltpu

D_MODEL = 1024
CHUNK = 64
PE_DIM = 256
HEADS = 4
HEAD_W = 128
DA_HD = 64
RT_KD = 128
D_FF = 2816
CONV_W = 3
N_BUCKETS = 32
MAX_DIST = 128
EPS = 1e-6
ROPE_BASE = 10000.0
SEG = HEADS * HEAD_W
D_IN = 7 * SEG + 2 * D_MODEL

MASK_VALUE = -1e30
V7X_VMEM_LIMIT = 56 * 1024 * 1024
ATTN_BLOCK = 256
F32 = jnp.float32
BF16 = jnp.bfloat16


def _cparams(*sem):
    return pltpu.CompilerParams(dimension_semantics=sem, vmem_limit_bytes=V7X_VMEM_LIMIT)


def _const_spec(shape):
    nd = len(shape)
    return pl.BlockSpec(shape, lambda *_: (0,) * nd, pipeline_mode=pl.Buffered(1))


def _rms(x, g):
    return x * lax.rsqrt(jnp.mean(x * x, axis=-1, keepdims=True) + EPS) * g


def _t5_bucket(rel):
    nb = N_BUCKETS // 2
    ret = jnp.where(rel > 0, nb, 0)
    n = jnp.abs(rel)
    max_exact = nb // 2
    nf = jnp.maximum(n, 1).astype(F32)
    large = max_exact + (jnp.log(nf / max_exact) / math.log(MAX_DIST / max_exact) * (nb - max_exact)).astype(jnp.int32)
    large = jnp.minimum(large, nb - 1)
    return ret + jnp.where(n < max_exact, n, large)


FAR_BUCKET = N_BUCKETS // 2 - 1
LOG2E = math.log2(math.e)


def _bias_kernel(idx_ref, madd_ref, rb_ref, out_ref):
    h = pl.program_id(0)
    idx = idx_ref[...]
    bias = jnp.zeros(idx.shape, F32)
    for b in range(N_BUCKETS):
        bias = jnp.where(idx == b, rb_ref[b, h], bias)
    out_ref[0] = (bias - rb_ref[FAR_BUCKET, h]) * LOG2E + madd_ref[...]


def _bias_tiles(rel_bias, qpos, kpos):
    rel = kpos[:, None, :] - qpos[:, :, None]
    idx = _t5_bucket(rel).astype(jnp.int32)
    madd = jnp.where((kpos[:, None, :] // CHUNK) <= (qpos[:, :, None] // CHUNK), 0.0, MASK_VALUE).astype(F32)
    n, r, c = idx.shape
    return pl.pallas_call(
        _bias_kernel,
        grid=(HEADS,),
        in_specs=[pl.BlockSpec((n, r, c), lambda h: (0, 0, 0)),
                  pl.BlockSpec((n, r, c), lambda h: (0, 0, 0)),
                  pl.BlockSpec(memory_space=pltpu.SMEM)],
        out_specs=pl.BlockSpec((1, n, r, c), lambda h: (h, 0, 0, 0)),
        out_shape=jax.ShapeDtypeStruct((HEADS, n, r, c), F32),
        compiler_params=_cparams("arbitrary"),
        name="bias_tiles",
    )(idx, madd, rel_bias)


INPROJ_PART = 256


def _inproj_kernel(x_ref, cos_ref, sin_ref, gmix_ref, w_ref, gq_ref, gk_ref,
                   q_ref, k32_ref, kb_ref, v32_ref, vb_ref, qr_ref, kr_ref, vr_ref, sg_ref, sd_ref, sr_ref):
    tm = x_ref.shape[0]
    nparts = tm // INPROJ_PART if tm % INPROJ_PART == 0 else 1
    parts = [slice(k * (tm // nparts), (k + 1) * (tm // nparts)) for k in range(nparts)]
    a = [_rms(x_ref[rs, :], gmix_ref[...]).astype(BF16) for rs in parts]

    def seg(k, c0, c1):
        return jnp.dot(a[k], w_ref[:, c0:c1], preferred_element_type=F32)

    def group_norm(z, g):
        outs = []
        for h in range(HEADS):
            zs = z[:, h * HEAD_W:(h + 1) * HEAD_W]
            sq = zs * zs
            low = lax.broadcasted_iota(jnp.int32, sq.shape, 1) < DA_HD
            s_all = jnp.sum(sq, axis=1, keepdims=True)
            s_low = jnp.sum(jnp.where(low, sq, 0.0), axis=1, keepdims=True)
            ms = jnp.where(low, s_low, s_all - s_low) * (1.0 / DA_HD)
            outs.append(zs * lax.rsqrt(ms + EPS))
        return jnp.concatenate(outs, axis=1) * g

    def rotary(z, rs):
        cos, sin = cos_ref[rs, :], sin_ref[rs, :]
        outs = []
        for h in range(HEADS):
            zs = z[:, h * HEAD_W:(h + 1) * HEAD_W]
            outs.append(zs * cos + pltpu.roll(zs, HEAD_W // 2, 1) * sin)
        return jnp.concatenate(outs, axis=1)

    def by_head(z, rs, out32_ref):
        for h in range(HEADS):
            rows = pl.ds(rs.start * HEADS + h, rs.stop - rs.start, stride=HEADS)
            out32_ref[rows, :] = z[:, h * HEAD_W:(h + 1) * HEAD_W]

    def put_q(z, rs):
        q_ref[rs, :] = (group_norm(z, gq_ref[...]) * (DA_HD ** -0.5 * LOG2E)).astype(BF16)

    def put_k(z, rs):
        k = group_norm(z, gk_ref[...])
        kb_ref[rs, :] = k.astype(BF16)
        by_head(k, rs, k32_ref)

    def put_v(z, rs):
        vb_ref[rs, :] = z.astype(BF16)
        by_head(z, rs, v32_ref)

    def put_qr(z, rs):
        qr_ref[rs, :] = rotary(z, rs).astype(BF16)

    def put_kr(z, rs):
        kr_ref[rs, :] = (rotary(z, rs) * RT_KD ** -0.5).astype(BF16)

    def put_vr(z, rs):
        vr_ref[rs, :] = z.astype(BF16)

    def put_sg(z, rs):
        sg_ref[rs, :] = (z * jax.nn.sigmoid(z)).astype(BF16)

    def put_sd(z, rs):
        sd_ref[rs, :] = jax.nn.sigmoid(z).astype(BF16)

    def put_sr(z, rs):
        sr_ref[rs, :] = jax.nn.sigmoid(z).astype(BF16)

    puts = [put_q, put_k, put_v, put_qr, put_kr, put_vr, put_sg, put_sd, put_sr]
    cols = [0, SEG, 2 * SEG, 3 * SEG, 4 * SEG, 5 * SEG, 6 * SEG, 7 * SEG, 7 * SEG + D_MODEL, D_IN]
    z_next = [seg(k, cols[0], cols[1]) for k in range(nparts)]
    for i, put in enumerate(puts):
        for k, rs in enumerate(parts):
            z = z_next[k]
            if i + 1 < len(puts):
                z_next[k] = seg(k, cols[i + 1], cols[i + 2])
            put(z, rs)


def _inproj(x2, cos_t, sin_t, gmix, w_in, gq, gk, tm):
    n = x2.shape[0]
    nt = cos_t.shape[0] // tm
    row = lambda w: pl.BlockSpec((tm, w), lambda i: (i, 0))
    tab = pl.BlockSpec((tm, HEAD_W), lambda i: (i % nt, 0))
    out_w = [(SEG, BF16), (None, F32), (SEG, BF16), (None, F32), (SEG, BF16), (SEG, BF16), (SEG, BF16),
             (SEG, BF16), (SEG, BF16), (D_MODEL, BF16), (D_MODEL, BF16)]
    by_head = pl.BlockSpec((tm * HEADS, HEAD_W), lambda i: (i, 0))
    return pl.pallas_call(
        _inproj_kernel,
        grid=(n // tm,),
        in_specs=[row(D_MODEL), tab, tab, _const_spec((1, D_MODEL)), _const_spec((D_MODEL, D_IN)),
                  _const_spec((1, SEG)), _const_spec((1, SEG))],
        out_specs=[by_head if w is None else row(w) for w, _ in out_w],
        out_shape=[jax.ShapeDtypeStruct((n * HEADS, HEAD_W) if w is None else (n, w), dt) for w, dt in out_w],
        compiler_params=_cparams("arbitrary"),
        name="inproj",
    )(x2, cos_t, sin_t, gmix, w_in, gq, gk)


def _split_maps(q):
    lane = lax.broadcasted_iota(jnp.int32, q.shape, 1)
    zero = jnp.zeros_like(q)
    return jnp.concatenate([jnp.where(lane < DA_HD, q, zero), jnp.where(lane >= DA_HD, q, zero)], axis=0)


def _flash_init(m_scr, l_scr, acc_scr):
    m_scr[...] = jnp.full(m_scr.shape, MASK_VALUE, F32)
    l_scr[...] = jnp.zeros(l_scr.shape, F32)
    acc_scr[...] = jnp.zeros(acc_scr.shape, F32)


def _flash_step(qq, k, v, bias, m_scr, l_scr, acc_scr):
    s = lax.dot_general(qq, k, (((1,), (1,)), ((), ())), preferred_element_type=F32)
    if bias is not None:
        s = s + bias
    m_prev = m_scr[...]
    m_new = jnp.maximum(m_prev, jnp.max(s, axis=1, keepdims=True))
    alpha = jnp.exp2(m_prev - m_new)
    p = jnp.exp2(s - m_new[:, :1])
    l_scr[...] = alpha * l_scr[...] + jnp.sum(p, axis=1, keepdims=True)
    acc_scr[...] = alpha * acc_scr[...] + jnp.dot(p.astype(BF16), v, preferred_element_type=F32)
    m_scr[...] = m_new


def _flash_step_units(qqs, ks, vs, biases, m_scr, l_scr, acc_scr):
    nt = (((1,), (1,)), ((), ()))
    units = range(len(qqs))
    ss = [lax.dot_general(qq, k, nt, preferred_element_type=F32) for qq, k in zip(qqs, ks)]
    ss = [s if b is None else s + b for s, b in zip(ss, biases)]
    m_prev = [m_scr[u] for u in units]
    l_prev = [l_scr[u] for u in units]
    acc_prev = [acc_scr[u] for u in units]
    m_new = [jnp.maximum(mp, jnp.max(s, axis=1, keepdims=True)) for mp, s in zip(m_prev, ss)]
    alpha = [jnp.exp2(mp - mn) for mp, mn in zip(m_prev, m_new)]
    ps = [jnp.exp2(s - mn[:, :1]) for s, mn in zip(ss, m_new)]
    pv = [jnp.dot(p.astype(BF16), v, preferred_element_type=F32) for p, v in zip(ps, vs)]
    for u in units:
        l_scr[u] = alpha[u] * l_prev[u] + jnp.sum(ps[u], axis=1, keepdims=True)
        acc_scr[u] = alpha[u] * acc_prev[u] + pv[u]
        m_scr[u] = m_new[u]


FLASH_STRIP = 64


def _flash_step_staged(qqs, ks, vs, bias_fn, first, m_scr, l_scr, acc_scr, s_scr, p_scr, a_scr):
    rows = s_scr.shape[1]
    tk = ks[0].shape[0]
    for h in range(HEADS):
        s_scr[h, :, :tk] = lax.dot_general(qqs[h], ks[h], (((1,), (1,)), ((), ())), preferred_element_type=F32)
        for r0 in range(0, rows, FLASH_STRIP):
            rs = slice(r0, r0 + FLASH_STRIP)
            s = s_scr[h, rs, :tk]
            if bias_fn is not None:
                s = s + bias_fn(h, r0, FLASH_STRIP)
            m_new = jnp.max(s, axis=1, keepdims=True)
            if first:
                m_new = jnp.broadcast_to(m_new, (FLASH_STRIP, HEAD_W))
            else:
                m_prev = m_scr[h, rs, :]
                m_new = jnp.maximum(m_prev, m_new)
                a_scr[h, rs, :] = jnp.exp2(m_prev - m_new)
            m_scr[h, rs, :] = m_new
            p_scr[h, rs, :tk] = jnp.exp2(s - jnp.concatenate([m_new] * (tk // HEAD_W), axis=1)).astype(BF16)
        v = vs[h]
        pv = jnp.dot(p_scr[h, :, :tk], jnp.concatenate([v, jnp.ones_like(v)], axis=1),
                     preferred_element_type=F32)
        if first:
            acc_scr[h] = pv[:, :HEAD_W]
            l_scr[h] = pv[:, HEAD_W:]
        else:
            alpha = a_scr[h]
            acc_scr[h] = alpha * acc_scr[h] + pv[:, :HEAD_W]
            l_scr[h] = alpha * l_scr[h] + pv[:, HEAD_W:]


def _lam(lamv_ref, lam_init):
    lv = lamv_ref[...]
    s1 = jnp.sum(lv[0:1] * lv[1:2], axis=1, keepdims=True)
    s2 = jnp.sum(lv[2:3] * lv[3:4], axis=1, keepdims=True)
    return jnp.exp(s1) - jnp.exp(s2) + lam_init


def _flash_finish(t, lam, lam_init, gda_ref, l_scr, acc_scr):
    o = acc_scr[...] / l_scr[...]
    o = o[:t] - lam * o[t:]
    return _rms(o, gda_ref[...]) * (1.0 - lam_init)


def _retention_step(qr, kr, vr, s_prev, dmat, dq, dk, dc):
    att = lax.dot_general(qr, kr, (((1,), (1,)), ((), ())), preferred_element_type=F32) * dmat
    o = jnp.dot(att.astype(BF16), vr, preferred_element_type=F32)
    o = o + jnp.dot(qr, s_prev.astype(BF16), preferred_element_type=F32) * dq
    kdec = (kr.astype(F32) * dk).astype(BF16)
    s_new = s_prev * dc + lax.dot_general(kdec, vr, (((0,), (0,)), ((), ())), preferred_element_type=F32)
    return o, s_new


def _decay_tables(c):
    log_g = jnp.log1p(-jnp.power(2.0, -5.0 - jnp.arange(HEADS, dtype=F32)))
    idx = jnp.arange(c, dtype=F32)
    diff = idx[:, None] - idx[None, :]
    dmat = jnp.where(diff >= 0, jnp.exp(log_g[:, None, None] * jnp.maximum(diff, 0.0)), 0.0)
    dq = jnp.broadcast_to(jnp.exp(log_g[:, None] * (idx + 1.0)[None, :])[:, :, None], (HEADS, c, HEAD_W))
    dk = jnp.broadcast_to(jnp.exp(log_g[:, None] * (c - 1.0 - idx)[None, :])[:, :, None], (HEADS, c, HEAD_W))
    dc = jnp.broadcast_to(jnp.exp(log_g * c)[:, None, None], (HEADS, 8, HEAD_W))
    return dmat.astype(F32), dq.astype(F32), dk.astype(F32), dc.astype(F32)


def _head(h):
    return slice(h * HEAD_W, (h + 1) * HEAD_W)


SEQ_GROUP = 2


def _mixer_prompt_kernel(lam_init, q_ref, k_ref, v_ref, bias_ref, qr_ref, kr_ref, vr_ref, sg_ref,
                         dmat_ref, dq_ref, dk_ref, dc_ref, s0_ref, lamv_ref, gda_ref, grt_ref,
                         od_ref, or_ref, sout_ref, m_scr, l_scr, acc_scr, sc_scr, p_scr, a_scr, s_scr):
    qb = pl.program_id(1)
    ng, t = q_ref.shape[0], q_ref.shape[1]
    qq = [[_split_maps(q_ref[g, :, _head(h)]) for h in range(HEADS)] for g in range(ng)]

    def key_step(kb, nblk, bias_cols, first=False):
        off = pl.multiple_of(kb * t, t)
        bias_fn = None if bias_cols is None else (lambda h, r0, n: bias_ref[h, 0, r0 % t:r0 % t + n, bias_cols])
        for g in range(ng):
            _flash_step_staged(qq[g], [k_ref[g, pl.ds(off, nblk * t), _head(h)] for h in range(HEADS)],
                               [v_ref[g, pl.ds(off, nblk * t), _head(h)] for h in range(HEADS)], bias_fn, first,
                               m_scr.at[g], l_scr.at[g], acc_scr.at[g], sc_scr.at[g], p_scr.at[g], a_scr.at[g])

    key_step(qb, 1, slice(t, 2 * t), first=True)

    @pl.when(qb >= 1)
    def _():
        key_step(qb - 1, 1, slice(0, t))

    n_far = jnp.maximum(qb - 1, 0)

    @pl.when(n_far % 2 == 1)
    def _():
        key_step(n_far - 1, 1, None)

    def far_body(i, carry):
        key_step(2 * i, 2, None)
        return carry

    lax.fori_loop(0, n_far // 2, far_body, 0)
    lam = _lam(lamv_ref, lam_init)

    @pl.when(qb == 0)
    def _():
        s_scr[...] = s0_ref[...]


    for g in range(ng):
        for h in range(HEADS):
            od_ref[g, :, _head(h)] = _flash_finish(
                t, lam, lam_init, gda_ref, l_scr.at[g, h], acc_scr.at[g, h]).astype(BF16)
            o, s_new = _retention_step(qr_ref[g, :, _head(h)], kr_ref[g, :, _head(h)], vr_ref[g, :, _head(h)],
                                       s_scr[g, h], dmat_ref[h], dq_ref[h], dk_ref[h], dc_ref[h, 0:1])
            s_scr[g, h] = s_new
            sout_ref[g, h] = s_new
            or_ref[g, :, _head(h)] = (_rms(o, grt_ref[...]) * sg_ref[g, :, _head(h)].astype(F32)).astype(BF16)


def _mixer_prompt(lam_init, b, t, q, kb, vb, bias, qr, kr, vr, sg, s0, lamv, gda, grt):
    tq = ATTN_BLOCK
    nq = t // tq
    ng = SEQ_GROUP if b % SEQ_GROUP == 0 else 1
    tables = _decay_tables(tq)
    by_seq = lambda a: a.reshape(b, t, SEG)
    tile = pl.BlockSpec((ng, tq, SEG), lambda bi, qb: (bi, qb, 0))
    seq = pl.BlockSpec((ng, t, SEG), lambda bi, qb: (bi, 0, 0))
    state = pl.BlockSpec((ng, HEADS, RT_KD, HEAD_W), lambda bi, qb: (bi, 0, 0, 0))
    stat = pltpu.VMEM((ng, HEADS, 2 * tq, HEAD_W), F32)
    od, orr, s_new = pl.pallas_call(
        functools.partial(_mixer_prompt_kernel, lam_init),
        grid=(b // ng, nq),
        in_specs=[tile, seq, seq, _const_spec(bias.shape), tile, tile, tile, tile]
                 + [_const_spec(a.shape) for a in tables]
                 + [state, _const_spec(lamv.shape), _const_spec(gda.shape), _const_spec(grt.shape)],
        out_specs=[tile, tile, state],
        out_shape=[jax.ShapeDtypeStruct((b, t, SEG), BF16), jax.ShapeDtypeStruct((b, t, SEG), BF16),
                   jax.ShapeDtypeStruct((b, HEADS, RT_KD, HEAD_W), F32)],
        scratch_shapes=[stat, stat, stat, pltpu.VMEM((ng, HEADS, 2 * tq, 2 * tq), F32),
                        pltpu.VMEM((ng, HEADS, 2 * tq, 2 * tq), BF16), stat,
                        pltpu.VMEM((ng, HEADS, RT_KD, HEAD_W), F32)],
        compiler_params=_cparams("arbitrary", "arbitrary"),
        name="mixer_prompt",
    )(by_seq(q), by_seq(kb), by_seq(vb), bias, by_seq(qr), by_seq(kr), by_seq(vr), by_seq(sg), *tables,
      s0, lamv, gda, grt)
    return od.reshape(b * t, SEG), orr.reshape(b * t, SEG), s_new


CACHE_CHUNK = 1024


def _mixer_sample_kernel(lam_init, q_ref, ck_ref, cv_ref, kn_ref, vn_ref, bnear_ref, bnew_ref,
                         qr_ref, kr_ref, vr_ref, sg_ref, dmat_ref, dq_ref, dk_ref, dc_ref, s0_ref,
                         lamv_ref, gda_ref, grt_ref, od_ref, or_ref, sout_ref, m_scr, l_scr, acc_scr):
    c = pl.program_id(1)
    last = pl.num_programs(1) - 1
    ng, t = q_ref.shape[0], q_ref.shape[1]
    units = [(g, h) for g in range(ng) for h in range(HEADS)]
    qq = [_split_maps(q_ref[g, :, _head(h)]) for g, h in units]

    @pl.when(c == 0)
    def _():
        _flash_init(m_scr, l_scr, acc_scr)

    def cache_keys(bias_ref):
        rows = [pl.ds(h, CACHE_CHUNK, stride=HEADS) for h in range(HEADS)]
        _flash_step_units(
            qq, [ck_ref[g, rows[h], :].astype(BF16) for g, h in units],
            [cv_ref[g, rows[h], :].astype(BF16) for g, h in units],
            [None if bias_ref is None else jnp.concatenate([bias_ref[h, 0]] * 2, axis=0) for g, h in units],
            m_scr, l_scr, acc_scr)

    @pl.when(c < last)
    def _():
        cache_keys(None)

    @pl.when(c == last)
    def _():
        cache_keys(bnear_ref)
        lam = _lam(lamv_ref, lam_init)
        _flash_step_units(
            qq, [kn_ref[g, :, _head(h)] for g, h in units], [vn_ref[g, :, _head(h)] for g, h in units],
            [jnp.concatenate([bnew_ref[h, 0]] * 2, axis=0) for g, h in units], m_scr, l_scr, acc_scr)
        for u, (g, h) in enumerate(units):
            od_ref[g, :, _head(h)] = _flash_finish(t, lam, lam_init, gda_ref, l_scr.at[u], acc_scr.at[u]).astype(BF16)
            o, s_new = _retention_step(qr_ref[g, :, _head(h)], kr_ref[g, :, _head(h)], vr_ref[g, :, _head(h)],
                                       s0_ref[g, h], dmat_ref[h], dq_ref[h], dk_ref[h], dc_ref[h, 0:1])
            sout_ref[g, h] = s_new
            or_ref[g, :, _head(h)] = (_rms(o, grt_ref[...]) * sg_ref[g, :, _head(h)].astype(F32)).astype(BF16)


def _mixer_sample(lam_init, layer, b, t, past, q, cache_k, cache_v, kn, vn, bnear, bnew, qr, kr, vr, sg, s0,
                  lamv, gda, grt):
    assert past % CACHE_CHUNK == 0
    ng = SEQ_GROUP if b % SEQ_GROUP == 0 else 1
    tables = _decay_tables(t)
    by_seq = lambda a: a.reshape(b, t, SEG)
    tile = pl.BlockSpec((ng, t, SEG), lambda bi, c: (bi, 0, 0))
    cache_k, cache_v = (a.reshape(-1, past * HEADS, HEAD_W) for a in (cache_k, cache_v))
    cache = pl.BlockSpec((ng, CACHE_CHUNK * HEADS, HEAD_W), lambda bi, c: (layer * (b // ng) + bi, c, 0))
    state = pl.BlockSpec((ng, HEADS, RT_KD, HEAD_W), lambda bi, c: (bi, 0, 0, 0))
    consts = [bnear, bnew]
    stat = pltpu.VMEM((ng * HEADS, 2 * t, HEAD_W), F32)
    od, orr, s_new = pl.pallas_call(
        functools.partial(_mixer_sample_kernel, lam_init),
        grid=(b // ng, past // CACHE_CHUNK),
        in_specs=[tile, cache, cache, tile, tile] + [_const_spec(a.shape) for a in consts]
                 + [tile, tile, tile, tile] + [_const_spec(a.shape) for a in tables]
                 + [state, _const_spec(lamv.shape), _const_spec(gda.shape), _const_spec(grt.shape)],
        out_specs=[tile, tile, state],
        out_shape=[jax.ShapeDtypeStruct((b, t, SEG), BF16), jax.ShapeDtypeStruct((b, t, SEG), BF16),
                   jax.ShapeDtypeStruct((b, HEADS, RT_KD, HEAD_W), F32)],
        scratch_shapes=[stat, stat, stat],
        compiler_params=_cparams("arbitrary", "arbitrary"),
        name="mixer_sample",
    )(by_seq(q), cache_k, cache_v, by_seq(kn), by_seq(vn), bnear, bnew, by_seq(qr), by_seq(kr), by_seq(vr),
      by_seq(sg), *tables, s0, lamv, gda, grt)
    return od.reshape(b * t, SEG), orr.reshape(b * t, SEG), s_new


FFN_CHUNK = 512
CHANNEL_PART = 256


def _channel_kernel(nseq, rows, tiles_per_seq, x_ref, od_ref, or_ref, sd_ref, sr_ref, p_ref, st_ref,
                    wbd_ref, wbr_ref, wo_ref, gffn_ref, wg_ref, wu_ref, cw_ref, cb_ref, wd_ref,
                    gpe_ref, wpe_ref, wpg_ref, y_ref, cn_ref, gbuf):
    i = pl.program_id(0)
    tm = x_ref.shape[0]
    dot = functools.partial(jnp.dot, preferred_element_type=F32)
    split = tm % CHANNEL_PART == 0 and (rows % CHANNEL_PART == 0 or CHANNEL_PART % rows == 0)
    nparts = tm // CHANNEL_PART if split else 1
    parts = [slice(k * (tm // nparts), (k + 1) * (tm // nparts)) for k in range(nparts)]

    mix_d = [dot(od_ref[rs, :], wbd_ref[...]) for rs in parts]
    mix_r = [dot(or_ref[rs, :], wbr_ref[...]) for rs in parts]
    h1 = [x_ref[rs, :] + dot((sd_ref[rs, :].astype(F32) * md + sr_ref[rs, :].astype(F32) * mr).astype(BF16),
                             wo_ref[...]) for rs, md, mr in zip(parts, mix_d, mix_r)]
    f = [_rms(h, gffn_ref[...]).astype(BF16) for h in h1]
    pe = [dot(p_ref[rs, :].astype(BF16), wpe_ref[...]) for rs in parts]

    if tiles_per_seq > 1:
        @pl.when(i % tiles_per_seq == 0)
        def _():
            gbuf[6:8, :] = st_ref[0]
    stride = rows + 8
    chunks = [slice(c0, min(c0 + FFN_CHUNK, D_FF)) for c0 in range(0, D_FF, FFN_CHUNK)]
    up = lambda k, cs: (dot(f[k], wg_ref[:, cs]), dot(f[k], wu_ref[:, cs]))
    nxt = [up(k, chunks[0]) for k in range(nparts)]
    h2 = list(h1)
    for ci, cs in enumerate(chunks):
        cw = cw_ref[:, cs]
        for k, rs in enumerate(parts):
            g, u = nxt[k]
            if ci + 1 < len(chunks):
                nxt[k] = up(k, chunks[ci + 1])
            n = min(rows, rs.stop - rs.start)
            convs = []
            for r0 in range(rs.start, rs.stop, n):
                j, a = r0 // rows, r0 % rows
                base = j * stride + a
                if tiles_per_seq == 1 and a == 0:
                    gbuf[base + 6:base + 8, cs] = st_ref[j, :, cs]
                gj = g[r0 - rs.start:r0 - rs.start + n]
                gbuf[base + 8:base + 8 + n, cs] = gj
                convs.append(cb_ref[:, cs] + gbuf[base + 6:base + 6 + n, cs] * cw[0:1]
                             + gbuf[base + 7:base + 7 + n, cs] * cw[1:2] + gj * cw[2:3])
                if a + n == rows:
                    last2 = gbuf[base + n + 6:base + n + 8, cs]
                    cn_ref[j, :, cs] = last2
                    if tiles_per_seq > 1:
                        gbuf[6:8, cs] = last2
            gc = convs[0] if len(convs) == 1 else jnp.concatenate(convs, axis=0)
            h2[k] = h2[k] + dot((jax.nn.gelu(gc) * u).astype(BF16), wd_ref[cs, :])

    for k, rs in enumerate(parts):
        gate = jax.nn.sigmoid(dot(_rms(h2[k], gpe_ref[...]).astype(BF16), wpg_ref[...]))
        y_ref[rs, :] = h2[k] + pe[k] * gate


def _channel(b, t, tm, x2, od, orr, sd, sr, p2, st, w):
    n = b * t
    rows = min(tm, t)
    nseq = tm // rows
    tiles_per_seq = t // rows
    row = lambda wd: pl.BlockSpec((tm, wd), lambda i: (i, 0))
    if tiles_per_seq == 1:
        st_spec = pl.BlockSpec((nseq, CONV_W - 1, D_FF), lambda i: (i, 0, 0))
    else:
        st_spec = pl.BlockSpec((1, CONV_W - 1, D_FF), lambda i: (i // tiles_per_seq, 0, 0))
    weights = [w["w_bd"], w["w_br"], w["w_o"], w["g_ffn"], w["w_g"], w["w_u"], w["conv_w"], w["conv_b"],
               w["w_d"], w["g_pe"], w["w_pe"], w["w_pg"]]
    return pl.pallas_call(
        functools.partial(_channel_kernel, nseq, rows, tiles_per_seq),
        grid=(n // tm,),
        in_specs=[row(D_MODEL), row(SEG), row(SEG), row(D_MODEL), row(D_MODEL), row(PE_DIM), st_spec]
                 + [_const_spec(a.shape) for a in weights],
        out_specs=[row(D_MODEL), st_spec],
        out_shape=[jax.ShapeDtypeStruct((n, D_MODEL), F32), jax.ShapeDtypeStruct((b, CONV_W - 1, D_FF), F32)],
        scratch_shapes=[pltpu.VMEM((nseq * (rows + 8), D_FF), F32)],
        compiler_params=_cparams("arbitrary"),
        name="channel",
    )(x2, od, orr, sd, sr, p2, st, *weights)


def _rope_tables(pos):
    half = HEAD_W // 2
    inv = jnp.power(ROPE_BASE, -jnp.arange(half, dtype=F32) / half)
    ang = pos.astype(F32)[:, None] * inv[None, :]
    cos, sin = jnp.cos(ang), jnp.sin(ang)
    return jnp.concatenate([cos, cos], axis=1), jnp.concatenate([-sin, sin], axis=1)


def _prep_weights(i, g_mix, w_in, g_q, g_k, lam_q1, lam_k1, lam_q2, lam_k2, g_da, g_rt, w_bd, w_br, w_o,
                  g_ffn, w_g, w_u, conv_w, conv_b, w_d, g_pe, w_pe, w_pg):
    return dict(
        g_mix=g_mix[i][None], w_in=w_in[i].astype(BF16),
        g_q=jnp.tile(g_q[i], SEG // DA_HD)[None], g_k=jnp.tile(g_k[i], SEG // DA_HD)[None],
        lamv=jnp.stack([lam_q1[i], lam_k1[i], lam_q2[i], lam_k2[i]]),
        g_da=g_da[i][None], g_rt=g_rt[i][None],
        w_bd=w_bd[i].astype(BF16), w_br=w_br[i].astype(BF16), w_o=w_o[i].astype(BF16),
        g_ffn=g_ffn[i][None], w_g=w_g[i].astype(BF16), w_u=w_u[i].astype(BF16),
        conv_w=conv_w[i], conv_b=conv_b[i][None], w_d=w_d[i].astype(BF16),
        g_pe=g_pe[i][None], w_pe=w_pe[i].astype(BF16), w_pg=w_pg[i].astype(BF16))


def _layer(h, pe, layer, cache_k, cache_v, s_ret, s_conv, rel_bias, lam_init, w):
    b, t, _ = h.shape
    n = b * t
    past = 0 if cache_k is None else cache_k.shape[2]
    x2 = h.reshape(n, D_MODEL)
    qpos = past + jnp.arange(t, dtype=jnp.int32)

    tm_in = min(512, n)
    cos_t, sin_t = _rope_tables(qpos)
    if t < tm_in:
        cos_t, sin_t = jnp.tile(cos_t, (tm_in // t, 1)), jnp.tile(sin_t, (tm_in // t, 1))
    q, k32, kb, v32, vb, qr, kr, vr, sg, sd, sr = _inproj(
        x2, cos_t, sin_t, w["g_mix"], w["w_in"], w["g_q"], w["g_k"], tm_in)

    if cache_k is None:
        tq = ATTN_BLOCK
        assert t % tq == 0 and tq % CHUNK == 0 and tq >= MAX_DIST
        bias = _bias_tiles(rel_bias, tq + jnp.arange(tq, dtype=jnp.int32)[None],
                           jnp.arange(2 * tq, dtype=jnp.int32)[None])
        s0 = jnp.zeros((b, HEADS, RT_KD, HEAD_W), F32)
        od, orr, s_new = _mixer_prompt(lam_init, b, t, q, kb, vb, bias, qr, kr, vr, sg, s0,
                                       w["lamv"], w["g_da"], w["g_rt"])
        st = jnp.zeros((b, CONV_W - 1, D_FF), F32)
    else:
        assert past % CHUNK == 0 and t <= CHUNK and CACHE_CHUNK >= MAX_DIST
        near = jnp.arange(past - CACHE_CHUNK, past, dtype=jnp.int32)
        bnear = _bias_tiles(rel_bias, qpos[None], near[None])
        bnew = _bias_tiles(rel_bias, qpos[None], qpos[None])
        od, orr, s_new = _mixer_sample(lam_init, layer, b, t, past, q, cache_k, cache_v, kb, vb, bnear, bnew,
                                       qr, kr, vr, sg, s_ret, w["lamv"], w["g_da"], w["g_rt"])
        st = s_conv

    tm_ch = min(512, n)
    y, conv_new = _channel(b, t, tm_ch, x2, od, orr, sd, sr, pe.reshape(n, PE_DIM), st, w)
    return (y.reshape(b, t, D_MODEL), k32.reshape(b, t, HEADS, HEAD_W), v32.reshape(b, t, HEADS, HEAD_W),
            s_new, conv_new)


def kernel(x_prompt, x_sample, p_prompt, p_sample, cache_k, cache_v, state_ret, state_conv, rel_bias, g_mix, w_in, g_q, g_k, lam_q1, lam_k1, lam_q2, lam_k2, g_da, g_rt, w_bd, w_br, w_o, g_ffn, w_g, w_u, conv_w, conv_b, w_d, g_pe, w_pe, w_pg):
    depth = w_in.shape[0]
    hp, hs = x_prompt, x_sample
    outs = [[] for _ in range(8)]
    for i in range(depth):
        lam_init = 0.8 - 0.6 * math.exp(-0.3 * i)
        w = _prep_weights(i, g_mix, w_in, g_q, g_k, lam_q1, lam_k1, lam_q2, lam_k2, g_da, g_rt, w_bd, w_br,
                          w_o, g_ffn, w_g, w_u, conv_w, conv_b, w_d, g_pe, w_pe, w_pg)
        hp, kp, vp, rp, cp = _layer(hp, p_prompt[i], i, None, None, None, None, rel_bias, lam_init, w)
        hs, ks, vs, rs, cs = _layer(hs, p_sample[i], i, cache_k, cache_v, state_ret[i], state_conv[i],
                                    rel_bias, lam_init, w)
        for lst, val in zip(outs, (kp, vp, rp, cp, ks, vs, rs, cs)):
            lst.append(val)
    return (hp, hs) + tuple(jnp.stack(o) for o in outs)
```

```python
import functools
import math

import jax
import jax.numpy as jnp
from jax import lax
from jax.experimental import pallas as pl
from jax.experimental.pallas import tpu as pltpu

D_MODEL = 1024
CHUNK = 64
PE_DIM = 256
HEADS = 4
HEAD_W = 128
DA_HD = 64
RT_KD = 128
D_FF = 2816
CONV_W = 3
N_BUCKETS = 32
MAX_DIST = 128
EPS = 1e-6
ROPE_BASE = 10000.0
SEG = HEADS * HEAD_W
D_IN = 7 * SEG + 2 * D_MODEL

MASK_VALUE = -1e30
V7X_VMEM_LIMIT = 56 * 1024 * 1024
ATTN_BLOCK = 256
F32 = jnp.float32
BF16 = jnp.bfloat16


def _cparams(*sem):
    return pltpu.CompilerParams(dimension_semantics=sem, vmem_limit_bytes=V7X_VMEM_LIMIT)


def _const_spec(shape):
    nd = len(shape)
    return pl.BlockSpec(shape, lambda *_: (0,) * nd, pipeline_mode=pl.Buffered(1))


def _rms(x, g):
    return x * lax.rsqrt(jnp.mean(x * x, axis=-1, keepdims=True) + EPS) * g


def _t5_bucket(rel):
    nb = N_BUCKETS // 2
    ret = jnp.where(rel > 0, nb, 0)
    n = jnp.abs(rel)
    max_exact = nb // 2
    nf = jnp.maximum(n, 1).astype(F32)
    large = max_exact + (jnp.log(nf / max_exact) / math.log(MAX_DIST / max_exact) * (nb - max_exact)).astype(jnp.int32)
    large = jnp.minimum(large, nb - 1)
    return ret + jnp.where(n < max_exact, n, large)


FAR_BUCKET = N_BUCKETS // 2 - 1
LOG2E = math.log2(math.e)


def _bias_kernel(idx_ref, madd_ref, rb_ref, out_ref):
    h = pl.program_id(0)
    idx = idx_ref[...]
    bias = jnp.zeros(idx.shape, F32)
    for b in range(N_BUCKETS):
        bias = jnp.where(idx == b, rb_ref[b, h], bias)
    out_ref[0] = (bias - rb_ref[FAR_BUCKET, h]) * LOG2E + madd_ref[...]


def _bias_tiles(rel_bias, qpos, kpos):
    rel = kpos[:, None, :] - qpos[:, :, None]
    idx = jnp.bitwise_and(_t5_bucket(rel).astype(jnp.int32), N_BUCKETS - 1)
    madd = jnp.where((kpos[:, None, :] // CHUNK) <= (qpos[:, :, None] // CHUNK), 0.0, MASK_VALUE).astype(F32)
    n, r, c = idx.shape
    return pl.pallas_call(
        _bias_kernel,
        grid=(HEADS,),
        in_specs=[pl.BlockSpec((n, r, c), lambda h: (0, 0, 0)),
                  pl.BlockSpec((n, r, c), lambda h: (0, 0, 0)),
                  pl.BlockSpec(memory_space=pltpu.SMEM)],
        out_specs=pl.BlockSpec((1, n, r, c), lambda h: (h, 0, 0, 0)),
        out_shape=jax.ShapeDtypeStruct((HEADS, n, r, c), F32),
        compiler_params=_cparams("arbitrary"),
        name="bias_tiles",
    )(idx, madd, rel_bias)


INPROJ_PART = 256


def _inproj_kernel(x_ref, cos_ref, sin_ref, gmix_ref, w_ref, gq_ref, gk_ref,
                   q_ref, k32_ref, kb_ref, v32_ref, vb_ref, qr_ref, kr_ref, vr_ref, sg_ref, sd_ref, sr_ref):
    tm = x_ref.shape[0]
    nparts = tm // INPROJ_PART if tm % INPROJ_PART == 0 else 1
    parts = [slice(k * (tm // nparts), (k + 1) * (tm // nparts)) for k in range(nparts)]
    a = [_rms(x_ref[rs, :], gmix_ref[...]).astype(BF16) for rs in parts]

    def seg(k, c0, c1):
        return jnp.dot(a[k], w_ref[:, c0:c1], preferred_element_type=F32)

    def group_norm(z, g):
        outs = []
        for h in range(HEADS):
            zs = z[:, h * HEAD_W:(h + 1) * HEAD_W]
            sq = zs * zs
            low = lax.broadcasted_iota(jnp.int32, sq.shape, 1) < DA_HD
            s_all = jnp.sum(sq, axis=1, keepdims=True)
            s_low = jnp.sum(jnp.where(low, sq, 0.0), axis=1, keepdims=True)
            ms = jnp.where(low, s_low, s_all - s_low) * (1.0 / DA_HD)
            outs.append(zs * lax.rsqrt(ms + EPS))
        return jnp.concatenate(outs, axis=1) * g

    def rotary(z, rs):
        cos, sin = cos_ref[rs, :], sin_ref[rs, :]
        outs = []
        for h in range(HEADS):
            zs = z[:, h * HEAD_W:(h + 1) * HEAD_W]
            outs.append(zs * cos + pltpu.roll(zs, HEAD_W // 2, 1) * sin)
        return jnp.concatenate(outs, axis=1)

    def by_head(z, rs, out32_ref):
        for h in range(HEADS):
            rows = pl.ds(rs.start * HEADS + h, rs.stop - rs.start, stride=HEADS)
            out32_ref[rows, :] = z[:, h * HEAD_W:(h + 1) * HEAD_W]

    def put_q(z, rs):
        q_ref[rs, :] = (group_norm(z, gq_ref[...]) * (DA_HD ** -0.5 * LOG2E)).astype(BF16)

    def put_k(z, rs):
        k = group_norm(z, gk_ref[...])
        kb_ref[rs, :] = k.astype(BF16)
        by_head(k, rs, k32_ref)

    def put_v(z, rs):
        vb_ref[rs, :] = z.astype(BF16)
        by_head(z, rs, v32_ref)

    def put_qr(z, rs):
        qr_ref[rs, :] = rotary(z, rs).astype(BF16)

    def put_kr(z, rs):
        kr_ref[rs, :] = (rotary(z, rs) * RT_KD ** -0.5).astype(BF16)

    def put_vr(z, rs):
        vr_ref[rs, :] = z.astype(BF16)

    def put_sg(z, rs):
        sg_ref[rs, :] = (z * jax.nn.sigmoid(z)).astype(BF16)

    def put_sd(z, rs):
        sd_ref[rs, :] = jax.nn.sigmoid(z).astype(BF16)

    def put_sr(z, rs):
        sr_ref[rs, :] = jax.nn.sigmoid(z).astype(BF16)

    puts = [put_q, put_k, put_v, put_qr, put_kr, put_vr, put_sg, put_sd, put_sr]
    cols = [0, SEG, 2 * SEG, 3 * SEG, 4 * SEG, 5 * SEG, 6 * SEG, 7 * SEG, 7 * SEG + D_MODEL, D_IN]
    z_next = [seg(k, cols[0], cols[1]) for k in range(nparts)]
    for i, put in enumerate(puts):
        for k, rs in enumerate(parts):
            z = z_next[k]
            if i + 1 < len(puts):
                z_next[k] = seg(k, cols[i + 1], cols[i + 2])
            put(z, rs)


def _inproj(x2, cos_t, sin_t, gmix, w_in, gq, gk, tm):
    n = x2.shape[0]
    nt = cos_t.shape[0] // tm
    row = lambda w: pl.BlockSpec((tm, w), lambda i: (i, 0))
    tab = pl.BlockSpec((tm, HEAD_W), lambda i: (i % nt, 0))
    out_w = [(SEG, BF16), (None, F32), (SEG, BF16), (None, F32), (SEG, BF16), (SEG, BF16), (SEG, BF16),
             (SEG, BF16), (SEG, BF16), (D_MODEL, BF16), (D_MODEL, BF16)]
    by_head = pl.BlockSpec((tm * HEADS, HEAD_W), lambda i: (i, 0))
    return pl.pallas_call(
        _inproj_kernel,
        grid=(n // tm,),
        in_specs=[row(D_MODEL), tab, tab, _const_spec((1, D_MODEL)), _const_spec((D_MODEL, D_IN)),
                  _const_spec((1, SEG)), _const_spec((1, SEG))],
        out_specs=[by_head if w is None else row(w) for w, _ in out_w],
        out_shape=[jax.ShapeDtypeStruct((n * HEADS, HEAD_W) if w is None else (n, w), dt) for w, dt in out_w],
        compiler_params=_cparams("arbitrary"),
        name="inproj",
    )(x2, cos_t, sin_t, gmix, w_in, gq, gk)


def _split_maps(q):
    lane = lax.broadcasted_iota(jnp.int32, q.shape, 1)
    zero = jnp.zeros_like(q)
    return jnp.concatenate([jnp.where(lane < DA_HD, q, zero), jnp.where(lane >= DA_HD, q, zero)], axis=0)


def _flash_init(m_scr, l_scr, acc_scr):
    m_scr[...] = jnp.full(m_scr.shape, MASK_VALUE, F32)
    l_scr[...] = jnp.zeros(l_scr.shape, F32)
    acc_scr[...] = jnp.zeros(acc_scr.shape, F32)


def _flash_step(qq, k, v, bias, m_scr, l_scr, acc_scr):
    s = lax.dot_general(qq, k, (((1,), (1,)), ((), ())), preferred_element_type=F32)
    if bias is not None:
        s = s + bias
    m_prev = m_scr[...]
    m_new = jnp.maximum(m_prev, jnp.max(s, axis=1, keepdims=True))
    alpha = jnp.exp2(m_prev - m_new)
    p = jnp.exp2(s - m_new[:, :1])
    l_scr[...] = alpha * l_scr[...] + jnp.sum(p, axis=1, keepdims=True)
    acc_scr[...] = alpha * acc_scr[...] + jnp.dot(p.astype(BF16), v, preferred_element_type=F32)
    m_scr[...] = m_new


def _flash_step_units(qqs, ks, vs, biases, m_scr, l_scr, acc_scr):
    nt = (((1,), (1,)), ((), ()))
    units = range(len(qqs))
    ss = [lax.dot_general(qq, k, nt, preferred_element_type=F32) for qq, k in zip(qqs, ks)]
    ss = [s if b is None else s + b for s, b in zip(ss, biases)]
    m_prev = [m_scr[u] for u in units]
    l_prev = [l_scr[u] for u in units]
    acc_prev = [acc_scr[u] for u in units]
    m_new = [jnp.maximum(mp, jnp.max(s, axis=1, keepdims=True)) for mp, s in zip(m_prev, ss)]
    alpha = [jnp.exp2(mp - mn) for mp, mn in zip(m_prev, m_new)]
    ps = [jnp.exp2(s - mn[:, :1]) for s, mn in zip(ss, m_new)]
    pv = [jnp.dot(p.astype(BF16), v, preferred_element_type=F32) for p, v in zip(ps, vs)]
    for u in units:
        l_scr[u] = alpha[u] * l_prev[u] + jnp.sum(ps[u], axis=1, keepdims=True)
        acc_scr[u] = alpha[u] * acc_prev[u] + pv[u]
        m_scr[u] = m_new[u]


FLASH_STRIP = 64


def _flash_step_staged(qqs, ks, vs, bias_fn, first, m_scr, l_scr, acc_scr, s_scr, p_scr, a_scr):
    rows = s_scr.shape[1]
    tk = ks[0].shape[0]
    for h in range(HEADS):
        s_scr[h, :, :tk] = lax.dot_general(qqs[h], ks[h], (((1,), (1,)), ((), ())), preferred_element_type=F32)
        for r0 in range(0, rows, FLASH_STRIP):
            rs = slice(r0, r0 + FLASH_STRIP)
            s = s_scr[h, rs, :tk]
            if bias_fn is not None:
                s = s + bias_fn(h, r0, FLASH_STRIP)
            m_new = jnp.max(s, axis=1, keepdims=True)
            if first:
                m_new = jnp.broadcast_to(m_new, (FLASH_STRIP, HEAD_W))
            else:
                m_prev = m_scr[h, rs, :]
                m_new = jnp.maximum(m_prev, m_new)
                a_scr[h, rs, :] = jnp.exp2(m_prev - m_new)
            m_scr[h, rs, :] = m_new
            p_scr[h, rs, :tk] = jnp.exp2(s - jnp.concatenate([m_new] * (tk // HEAD_W), axis=1)).astype(BF16)
        v = vs[h]
        pv = jnp.dot(p_scr[h, :, :tk], jnp.concatenate([v, jnp.ones_like(v)], axis=1),
                     preferred_element_type=F32)
        if first:
            acc_scr[h] = pv[:, :HEAD_W]
            l_scr[h] = pv[:, HEAD_W:]
        else:
            alpha = a_scr[h]
            acc_scr[h] = alpha * acc_scr[h] + pv[:, :HEAD_W]
            l_scr[h] = alpha * l_scr[h] + pv[:, HEAD_W:]


def _lam(lamv_ref, lam_init):
    lv = lamv_ref[...]
    s1 = jnp.sum(lv[0:1] * lv[1:2], axis=1, keepdims=True)
    s2 = jnp.sum(lv[2:3] * lv[3:4], axis=1, keepdims=True)
    return jnp.exp(s1) - jnp.exp(s2) + lam_init


def _flash_finish(t, lam, lam_init, gda_ref, l_scr, acc_scr):
    o = acc_scr[...] / l_scr[...]
    o = o[:t] - lam * o[t:]
    return _rms(o, gda_ref[...]) * (1.0 - lam_init)


def _retention_step(qr, kr, vr, s_prev, dmat, dq, dk, dc):
    att = lax.dot_general(qr, kr, (((1,), (1,)), ((), ())), preferred_element_type=F32) * dmat
    o = jnp.dot(att.astype(BF16), vr, preferred_element_type=F32)
    o = o + jnp.dot(qr, s_prev.astype(BF16), preferred_element_type=F32) * dq
    kdec = (kr.astype(F32) * dk).astype(BF16)
    s_new = s_prev * dc + lax.dot_general(kdec, vr, (((0,), (0,)), ((), ())), preferred_element_type=F32)
    return o, s_new


def _decay_tables(c):
    log_g = jnp.log1p(-jnp.power(2.0, -5.0 - jnp.arange(HEADS, dtype=F32)))
    idx = jnp.arange(c, dtype=F32)
    diff = idx[:, None] - idx[None, :]
    dmat = jnp.where(diff >= 0, jnp.exp(log_g[:, None, None] * jnp.maximum(diff, 0.0)), 0.0)
    dq = jnp.broadcast_to(jnp.exp(log_g[:, None] * (idx + 1.0)[None, :])[:, :, None], (HEADS, c, HEAD_W))
    dk = jnp.broadcast_to(jnp.exp(log_g[:, None] * (c - 1.0 - idx)[None, :])[:, :, None], (HEADS, c, HEAD_W))
    dc = jnp.broadcast_to(jnp.exp(log_g * c)[:, None, None], (HEADS, 8, HEAD_W))
    return dmat.astype(F32), dq.astype(F32), dk.astype(F32), dc.astype(F32)


def _head(h):
    return slice(h * HEAD_W, (h + 1) * HEAD_W)


SEQ_GROUP = 2


def _mixer_prompt_kernel(lam_init, q_ref, k_ref, v_ref, bias_ref, qr_ref, kr_ref, vr_ref, sg_ref,
                         dmat_ref, dq_ref, dk_ref, dc_ref, s0_ref, lamv_ref, gda_ref, grt_ref,
                         od_ref, or_ref, sout_ref, m_scr, l_scr, acc_scr, sc_scr, p_scr, a_scr, s_scr):
    qb = pl.program_id(1)
    ng, t = q_ref.shape[0], q_ref.shape[1]
    qq = [[_split_maps(q_ref[g, :, _head(h)]) for h in range(HEADS)] for g in range(ng)]

    def key_step(kb, nblk, bias_cols, first=False):
        off = pl.multiple_of(kb * t, t)
        bias_fn = None if bias_cols is None else (lambda h, r0, n: bias_ref[h, 0, r0 % t:r0 % t + n, bias_cols])
        for g in range(ng):
            _flash_step_staged(qq[g], [k_ref[g, pl.ds(off, nblk * t), _head(h)] for h in range(HEADS)],
                               [v_ref[g, pl.ds(off, nblk * t), _head(h)] for h in range(HEADS)], bias_fn, first,
                               m_scr.at[g], l_scr.at[g], acc_scr.at[g], sc_scr.at[g], p_scr.at[g], a_scr.at[g])

    key_step(qb, 1, slice(t, 2 * t), first=True)

    @pl.when(qb >= 1)
    def _():
        key_step(qb - 1, 1, slice(0, t))

    n_far = jnp.maximum(qb - 1, 0)

    @pl.when(n_far % 2 == 1)
    def _():
        key_step(n_far - 1, 1, None)

    def far_body(i, carry):
        key_step(2 * i, 2, None)
        return carry

    lax.fori_loop(0, n_far // 2, far_body, 0)
    lam = _lam(lamv_ref, lam_init)

    @pl.when(qb == 0)
    def _():
        s_scr[...] = s0_ref[...]


    for g in range(ng):
        for h in range(HEADS):
            od_ref[g, :, _head(h)] = _flash_finish(
                t, lam, lam_init, gda_ref, l_scr.at[g, h], acc_scr.at[g, h]).astype(BF16)
            o, s_new = _retention_step(qr_ref[g, :, _head(h)], kr_ref[g, :, _head(h)], vr_ref[g, :, _head(h)],
                                       s_scr[g, h], dmat_ref[h], dq_ref[h], dk_ref[h], dc_ref[h, 0:1])
            s_scr[g, h] = s_new
            sout_ref[g, h] = s_new
            or_ref[g, :, _head(h)] = (_rms(o, grt_ref[...]) * sg_ref[g, :, _head(h)].astype(F32)).astype(BF16)


def _mixer_prompt(lam_init, b, t, q, kb, vb, bias, qr, kr, vr, sg, s0, lamv, gda, grt):
    tq = ATTN_BLOCK
    nq = t // tq
    ng = SEQ_GROUP if b % SEQ_GROUP == 0 else 1
    tables = _decay_tables(tq)
    by_seq = lambda a: a.reshape(b, t, SEG)
    tile = pl.BlockSpec((ng, tq, SEG), lambda bi, qb: (bi, qb, 0))
    seq = pl.BlockSpec((ng, t, SEG), lambda bi, qb: (bi, 0, 0))
    state = pl.BlockSpec((ng, HEADS, RT_KD, HEAD_W), lambda bi, qb: (bi, 0, 0, 0))
    stat = pltpu.VMEM((ng, HEADS, 2 * tq, HEAD_W), F32)
    od, orr, s_new = pl.pallas_call(
        functools.partial(_mixer_prompt_kernel, lam_init),
        grid=(b // ng, nq),
        in_specs=[tile, seq, seq, _const_spec(bias.shape), tile, tile, tile, tile]
                 + [_const_spec(a.shape) for a in tables]
                 + [state, _const_spec(lamv.shape), _const_spec(gda.shape), _const_spec(grt.shape)],
        out_specs=[tile, tile, state],
        out_shape=[jax.ShapeDtypeStruct((b, t, SEG), BF16), jax.ShapeDtypeStruct((b, t, SEG), BF16),
                   jax.ShapeDtypeStruct((b, HEADS, RT_KD, HEAD_W), F32)],
        scratch_shapes=[stat, stat, stat, pltpu.VMEM((ng, HEADS, 2 * tq, 2 * tq), F32),
                        pltpu.VMEM((ng, HEADS, 2 * tq, 2 * tq), BF16), stat,
                        pltpu.VMEM((ng, HEADS, RT_KD, HEAD_W), F32)],
        compiler_params=_cparams("arbitrary", "arbitrary"),
        name="mixer_prompt",
    )(by_seq(q), by_seq(kb), by_seq(vb), bias, by_seq(qr), by_seq(kr), by_seq(vr), by_seq(sg), *tables,
      s0, lamv, gda, grt)
    return od.reshape(b * t, SEG), orr.reshape(b * t, SEG), s_new


CACHE_CHUNK = 2048


def _mixer_sample_kernel(lam_init, q_ref, ck_ref, cv_ref, kn_ref, vn_ref, bnear_ref, bnew_ref,
                         qr_ref, kr_ref, vr_ref, sg_ref, dmat_ref, dq_ref, dk_ref, dc_ref, s0_ref,
                         lamv_ref, gda_ref, grt_ref, od_ref, or_ref, sout_ref, m_scr, l_scr, acc_scr):
    c = pl.program_id(1)
    last = pl.num_programs(1) - 1
    ng, t = q_ref.shape[0], q_ref.shape[1]
    units = [(g, h) for g in range(ng) for h in range(HEADS)]
    qq = [_split_maps(q_ref[g, :, _head(h)]) for g, h in units]

    @pl.when(c == 0)
    def _():
        _flash_init(m_scr, l_scr, acc_scr)

    def cache_keys(bias_ref):
        rows = [pl.ds(h, CACHE_CHUNK, stride=HEADS) for h in range(HEADS)]
        _flash_step_units(
            qq, [ck_ref[g, rows[h], :].astype(BF16) for g, h in units],
            [cv_ref[g, rows[h], :].astype(BF16) for g, h in units],
            [None if bias_ref is None else jnp.concatenate([bias_ref[h, 0]] * 2, axis=0) for g, h in units],
            m_scr, l_scr, acc_scr)

    @pl.when(c < last)
    def _():
        cache_keys(None)

    @pl.when(c == last)
    def _():
        cache_keys(bnear_ref)
        lam = _lam(lamv_ref, lam_init)
        _flash_step_units(
            qq, [kn_ref[g, :, _head(h)] for g, h in units], [vn_ref[g, :, _head(h)] for g, h in units],
            [jnp.concatenate([bnew_ref[h, 0]] * 2, axis=0) for g, h in units], m_scr, l_scr, acc_scr)
        for u, (g, h) in enumerate(units):
            od_ref[g, :, _head(h)] = _flash_finish(t, lam, lam_init, gda_ref, l_scr.at[u], acc_scr.at[u]).astype(BF16)
            o, s_new = _retention_step(qr_ref[g, :, _head(h)], kr_ref[g, :, _head(h)], vr_ref[g, :, _head(h)],
                                       s0_ref[g, h], dmat_ref[h], dq_ref[h], dk_ref[h], dc_ref[h, 0:1])
            sout_ref[g, h] = s_new
            or_ref[g, :, _head(h)] = (_rms(o, grt_ref[...]) * sg_ref[g, :, _head(h)].astype(F32)).astype(BF16)


def _mixer_sample(lam_init, layer, b, t, past, q, cache_k, cache_v, kn, vn, bnear, bnew, qr, kr, vr, sg, s0,
                  lamv, gda, grt):
    assert past % CACHE_CHUNK == 0
    ng = SEQ_GROUP if b % SEQ_GROUP == 0 else 1
    tables = _decay_tables(t)
    by_seq = lambda a: a.reshape(b, t, SEG)
    tile = pl.BlockSpec((ng, t, SEG), lambda bi, c: (bi, 0, 0))
    cache_k, cache_v = (a.reshape(-1, past * HEADS, HEAD_W) for a in (cache_k, cache_v))
    cache = pl.BlockSpec((ng, CACHE_CHUNK * HEADS, HEAD_W), lambda bi, c: (layer * (b // ng) + bi, c, 0))
    state = pl.BlockSpec((ng, HEADS, RT_KD, HEAD_W), lambda bi, c: (bi, 0, 0, 0))
    consts = [bnear, bnew]
    stat = pltpu.VMEM((ng * HEADS, 2 * t, HEAD_W), F32)
    od, orr, s_new = pl.pallas_call(
        functools.partial(_mixer_sample_kernel, lam_init),
        grid=(b // ng, past // CACHE_CHUNK),
        in_specs=[tile, cache, cache, tile, tile] + [_const_spec(a.shape) for a in consts]
                 + [tile, tile, tile, tile] + [_const_spec(a.shape) for a in tables]
                 + [state, _const_spec(lamv.shape), _const_spec(gda.shape), _const_spec(grt.shape)],
        out_specs=[tile, tile, state],
        out_shape=[jax.ShapeDtypeStruct((b, t, SEG), BF16), jax.ShapeDtypeStruct((b, t, SEG), BF16),
                   jax.ShapeDtypeStruct((b, HEADS, RT_KD, HEAD_W), F32)],
        scratch_shapes=[stat, stat, stat],
        compiler_params=_cparams("arbitrary", "arbitrary"),
        name="mixer_sample",
    )(by_seq(q), cache_k, cache_v, by_seq(kn), by_seq(vn), bnear, bnew, by_seq(qr), by_seq(kr), by_seq(vr),
      by_seq(sg), *tables, s0, lamv, gda, grt)
    return od.reshape(b * t, SEG), orr.reshape(b * t, SEG), s_new


FFN_CHUNK = 1024
CHANNEL_PART = 256
CONV_HALO = CONV_W - 1
CONV_PAD = 8


def _channel_kernel(nseq, rows, tiles_per_seq, x_ref, od_ref, or_ref, sd_ref, sr_ref, p_ref, st_ref,
                    wbd_ref, wbr_ref, wo_ref, gffn_ref, wg_ref, wu_ref, cw_ref, cb_ref, wd_ref,
                    gpe_ref, wpe_ref, wpg_ref, y_ref, cn_ref, gbuf):
    i = pl.program_id(0)
    tm = x_ref.shape[0]
    dot = functools.partial(jnp.dot, preferred_element_type=F32)
    split = tm % CHANNEL_PART == 0 and (rows % CHANNEL_PART == 0 or CHANNEL_PART % rows == 0)
    nparts = tm // CHANNEL_PART if split else 1
    parts = [slice(k * (tm // nparts), (k + 1) * (tm // nparts)) for k in range(nparts)]

    mix_d = [dot(od_ref[rs, :], wbd_ref[...]) for rs in parts]
    mix_r = [dot(or_ref[rs, :], wbr_ref[...]) for rs in parts]
    h1 = [x_ref[rs, :] + dot((sd_ref[rs, :].astype(F32) * md + sr_ref[rs, :].astype(F32) * mr).astype(BF16),
                             wo_ref[...]) for rs, md, mr in zip(parts, mix_d, mix_r)]
    f = [_rms(h, gffn_ref[...]).astype(BF16) for h in h1]
    pe = [dot(p_ref[rs, :].astype(BF16), wpe_ref[...]) for rs in parts]

    halo0 = CONV_PAD - CONV_HALO
    if tiles_per_seq > 1:
        @pl.when(i % tiles_per_seq == 0)
        def _():
            gbuf[halo0:CONV_PAD, :] = st_ref[0]
    stride = rows + CONV_PAD
    chunks = [slice(c0, min(c0 + FFN_CHUNK, D_FF)) for c0 in range(0, D_FF, FFN_CHUNK)]
    up = lambda k, cs: (dot(f[k], wg_ref[:, cs]), dot(f[k], wu_ref[:, cs]))
    nxt = [up(k, chunks[0]) for k in range(nparts)]
    h2 = list(h1)
    for ci, cs in enumerate(chunks):
        cw = cw_ref[:, cs]
        for k, rs in enumerate(parts):
            g, u = nxt[k]
            if ci + 1 < len(chunks):
                nxt[k] = up(k, chunks[ci + 1])
            n = min(rows, rs.stop - rs.start)
            convs = []
            for r0 in range(rs.start, rs.stop, n):
                j, a = r0 // rows, r0 % rows
                base = j * stride + a
                if tiles_per_seq == 1 and a == 0:
                    gbuf[base + halo0:base + CONV_PAD, cs] = st_ref[j, :, cs]
                gj = g[r0 - rs.start:r0 - rs.start + n]
                gbuf[base + CONV_PAD:base + CONV_PAD + n, cs] = gj
                gc = cb_ref[:, cs]
                for tap in range(CONV_HALO):
                    gc = gc + gbuf[base + halo0 + tap:base + halo0 + tap + n, cs] * cw[tap:tap + 1]
                convs.append(gc + gj * cw[CONV_HALO:CONV_W])
                if a + n == rows:
                    last = gbuf[base + n + halo0:base + n + CONV_PAD, cs]
                    cn_ref[j, :, cs] = last
                    if tiles_per_seq > 1:
                        gbuf[halo0:CONV_PAD, cs] = last
            gc = convs[0] if len(convs) == 1 else jnp.concatenate(convs, axis=0)
            h2[k] = h2[k] + dot((jax.nn.gelu(gc) * u).astype(BF16), wd_ref[cs, :])

    for k, rs in enumerate(parts):
        gate = jax.nn.sigmoid(dot(_rms(h2[k], gpe_ref[...]).astype(BF16), wpg_ref[...]))
        y_ref[rs, :] = h2[k] + pe[k] * gate


def _channel(b, t, tm, x2, od, orr, sd, sr, p2, st, w):
    n = b * t
    rows = min(tm, t)
    nseq = tm // rows
    tiles_per_seq = t // rows
    row = lambda wd: pl.BlockSpec((tm, wd), lambda i: (i, 0))
    if tiles_per_seq == 1:
        st_spec = pl.BlockSpec((nseq, CONV_HALO, D_FF), lambda i: (i, 0, 0))
    else:
        st_spec = pl.BlockSpec((1, CONV_HALO, D_FF), lambda i: (i // tiles_per_seq, 0, 0))
    weights = [w["w_bd"], w["w_br"], w["w_o"], w["g_ffn"], w["w_g"], w["w_u"], w["conv_w"], w["conv_b"],
               w["w_d"], w["g_pe"], w["w_pe"], w["w_pg"]]
    return pl.pallas_call(
        functools.partial(_channel_kernel, nseq, rows, tiles_per_seq),
        grid=(n // tm,),
        in_specs=[row(D_MODEL), row(SEG), row(SEG), row(D_MODEL), row(D_MODEL), row(PE_DIM), st_spec]
                 + [_const_spec(a.shape) for a in weights],
        out_specs=[row(D_MODEL), st_spec],
        out_shape=[jax.ShapeDtypeStruct((n, D_MODEL), F32), jax.ShapeDtypeStruct((b, CONV_HALO, D_FF), F32)],
        scratch_shapes=[pltpu.VMEM((nseq * (rows + CONV_PAD), D_FF), F32)],
        compiler_params=_cparams("arbitrary"),
        name="channel",
    )(x2, od, orr, sd, sr, p2, st, *weights)


def _rope_tables(pos):
    half = HEAD_W // 2
    inv = jnp.power(ROPE_BASE, -jnp.arange(half, dtype=F32) / half)
    ang = pos.astype(F32)[:, None] * inv[None, :]
    cos, sin = jnp.cos(ang), jnp.sin(ang)
    return jnp.concatenate([cos, cos], axis=1), jnp.concatenate([-sin, sin], axis=1)


def _prep_weights(i, g_mix, w_in, g_q, g_k, lam_q1, lam_k1, lam_q2, lam_k2, g_da, g_rt, w_bd, w_br, w_o,
                  g_ffn, w_g, w_u, conv_w, conv_b, w_d, g_pe, w_pe, w_pg):
    return dict(
        g_mix=g_mix[i][None], w_in=w_in[i].astype(BF16),
        g_q=jnp.tile(g_q[i], SEG // DA_HD)[None], g_k=jnp.tile(g_k[i], SEG // DA_HD)[None],
        lamv=jnp.stack([lam_q1[i], lam_k1[i], lam_q2[i], lam_k2[i]]),
        g_da=g_da[i][None], g_rt=g_rt[i][None],
        w_bd=w_bd[i].astype(BF16), w_br=w_br[i].astype(BF16), w_o=w_o[i].astype(BF16),
        g_ffn=g_ffn[i][None], w_g=w_g[i].astype(BF16), w_u=w_u[i].astype(BF16),
        conv_w=conv_w[i], conv_b=conv_b[i][None], w_d=w_d[i].astype(BF16),
        g_pe=g_pe[i][None], w_pe=w_pe[i].astype(BF16), w_pg=w_pg[i].astype(BF16))


def _layer(h, pe, layer, cache_k, cache_v, s_ret, s_conv, rel_bias, lam_init, w):
    b, t, _ = h.shape
    n = b * t
    past = 0 if cache_k is None else cache_k.shape[2]
    x2 = h.reshape(n, D_MODEL)
    qpos = past + jnp.arange(t, dtype=jnp.int32)

    tm_in = min(512, n)
    cos_t, sin_t = _rope_tables(qpos)
    if t < tm_in:
        cos_t, sin_t = jnp.tile(cos_t, (tm_in // t, 1)), jnp.tile(sin_t, (tm_in // t, 1))
    q, k32, kb, v32, vb, qr, kr, vr, sg, sd, sr = _inproj(
        x2, cos_t, sin_t, w["g_mix"], w["w_in"], w["g_q"], w["g_k"], tm_in)

    if cache_k is None:
        tq = ATTN_BLOCK
        assert t % tq == 0 and tq % CHUNK == 0 and tq >= MAX_DIST
        bias = _bias_tiles(rel_bias, tq + jnp.arange(tq, dtype=jnp.int32)[None],
                           jnp.arange(2 * tq, dtype=jnp.int32)[None])
        s0 = jnp.zeros((b, HEADS, RT_KD, HEAD_W), F32)
        od, orr, s_new = _mixer_prompt(lam_init, b, t, q, kb, vb, bias, qr, kr, vr, sg, s0,
                                       w["lamv"], w["g_da"], w["g_rt"])
        st = jnp.zeros((b, CONV_HALO, D_FF), F32)
    else:
        assert past % CHUNK == 0 and t <= CHUNK and CACHE_CHUNK >= MAX_DIST
        near = jnp.arange(past - CACHE_CHUNK, past, dtype=jnp.int32)
        bnear = _bias_tiles(rel_bias, qpos[None], near[None])
        bnew = _bias_tiles(rel_bias, qpos[None], qpos[None])
        od, orr, s_new = _mixer_sample(lam_init, layer, b, t, past, q, cache_k, cache_v, kb, vb, bnear, bnew,
                                       qr, kr, vr, sg, s_ret, w["lamv"], w["g_da"], w["g_rt"])
        st = s_conv

    tm_ch = min(512, n)
    y, conv_new = _channel(b, t, tm_ch, x2, od, orr, sd, sr, pe.reshape(n, PE_DIM), st, w)
    return (y.reshape(b, t, D_MODEL), k32.reshape(b, t, HEADS, HEAD_W), v32.reshape(b, t, HEADS, HEAD_W),
            s_new, conv_new)


def kernel(x_prompt, x_sample, p_prompt, p_sample, cache_k, cache_v, state_ret, state_conv, rel_bias, g_mix, w_in, g_q, g_k, lam_q1, lam_k1, lam_q2, lam_k2, g_da, g_rt, w_bd, w_br, w_o, g_ffn, w_g, w_u, conv_w, conv_b, w_d, g_pe, w_pe, w_pg):
    depth = w_in.shape[0]
    hp, hs = x_prompt, x_sample
    outs = [[] for _ in range(8)]
    for i in range(depth):
        lam_init = 0.8 - 0.6 * math.exp(-0.3 * i)
        w = _prep_weights(i, g_mix, w_in, g_q, g_k, lam_q1, lam_k1, lam_q2, lam_k2, g_da, g_rt, w_bd, w_br,
                          w_o, g_ffn, w_g, w_u, conv_w, conv_b, w_d, g_pe, w_pe, w_pg)
        hp, kp, vp, rp, cp = _layer(hp, p_prompt[i], i, None, None, None, None, rel_bias, lam_init, w)
        hs, ks, vs, rs, cs = _layer(hs, p_sample[i], i, cache_k, cache_v, state_ret[i], state_conv[i],
                                    rel_bias, lam_init, w)
        for lst, val in zip(outs, (kp, vp, rp, cp, ks, vs, rs, cs)):
            lst.append(val)
    return (hp, hs) + tuple(jnp.stack(o) for o in outs)
```

```python
import functools
import math

import jax
import jax.numpy as jnp
from jax import lax
from jax.experimental import pallas as pl
from jax.experimental.pallas import tpu as pltpu

D_MODEL = 1024
CHUNK = 64
PE_DIM = 256
HEADS = 4
HEAD_W = 128
DA_HD = 64
RT_KD = 128
D_FF = 2816
CONV_W = 3
N_BUCKETS = 32
MAX_DIST = 128
EPS = 1e-6
ROPE_BASE = 10000.0
SEG = HEADS * HEAD_W
D_IN = 7 * SEG + 2 * D_MODEL

MASK_VALUE = -1e30
V7X_VMEM_LIMIT = 56 * 1024 * 1024
ATTN_BLOCK = 256
F32 = jnp.float32
BF16 = jnp.bfloat16


def _cparams(*sem):
    return pltpu.CompilerParams(dimension_semantics=sem, vmem_limit_bytes=V7X_VMEM_LIMIT)


def _const_spec(shape):
    nd = len(shape)
    return pl.BlockSpec(shape, lambda *_: (0,) * nd, pipeline_mode=pl.Buffered(1))


def _rms(x, g):
    return x * lax.rsqrt(jnp.mean(x * x, axis=-1, keepdims=True) + EPS) * g


def _t5_bucket(rel):
    nb = N_BUCKETS // 2
    ret = jnp.where(rel > 0, nb, 0)
    n = jnp.abs(rel)
    max_exact = nb // 2
    nf = jnp.maximum(n, 1).astype(F32)
    large = max_exact + (jnp.log(nf / max_exact) / math.log(MAX_DIST / max_exact) * (nb - max_exact)).astype(jnp.int32)
    large = jnp.minimum(large, nb - 1)
    return ret + jnp.where(n < max_exact, n, large)


FAR_BUCKET = N_BUCKETS // 2 - 1
LOG2E = math.log2(math.e)


def _bias_kernel(idx_ref, madd_ref, rb_ref, out_ref):
    h = pl.program_id(0)
    idx = idx_ref[...]
    bias = jnp.zeros(idx.shape, F32)
    for b in range(N_BUCKETS):
        bias = jnp.where(idx == b, rb_ref[b, h], bias)
    out_ref[0] = (bias - rb_ref[FAR_BUCKET, h]) * LOG2E + madd_ref[...]


def _bias_tiles(rel_bias, qpos, kpos):
    rel = kpos[:, None, :] - qpos[:, :, None]
    idx = jnp.bitwise_and(_t5_bucket(rel).astype(jnp.int32), N_BUCKETS - 1)
    madd = jnp.where((kpos[:, None, :] // CHUNK) <= (qpos[:, :, None] // CHUNK), 0.0, MASK_VALUE).astype(F32)
    n, r, c = idx.shape
    return pl.pallas_call(
        _bias_kernel,
        grid=(HEADS,),
        in_specs=[pl.BlockSpec((n, r, c), lambda h: (0, 0, 0)),
                  pl.BlockSpec((n, r, c), lambda h: (0, 0, 0)),
                  pl.BlockSpec(memory_space=pltpu.SMEM)],
        out_specs=pl.BlockSpec((1, n, r, c), lambda h: (h, 0, 0, 0)),
        out_shape=jax.ShapeDtypeStruct((HEADS, n, r, c), F32),
        compiler_params=_cparams("arbitrary"),
        name="bias_tiles",
    )(idx, madd, rel_bias)


INPROJ_PART = 256


def _inproj_kernel(x_ref, cos_ref, sin_ref, gmix_ref, w_ref, gq_ref, gk_ref,
                   q_ref, k32_ref, kb_ref, v32_ref, vb_ref, qr_ref, kr_ref, vr_ref, sg_ref, sd_ref, sr_ref):
    tm = x_ref.shape[0]
    nparts = tm // INPROJ_PART if tm % INPROJ_PART == 0 else 1
    parts = [slice(k * (tm // nparts), (k + 1) * (tm // nparts)) for k in range(nparts)]
    a = [_rms(x_ref[rs, :], gmix_ref[...]).astype(BF16) for rs in parts]

    def seg(k, c0, c1):
        return jnp.dot(a[k], w_ref[:, c0:c1], preferred_element_type=F32)

    def group_norm(z, g):
        outs = []
        for h in range(HEADS):
            zs = z[:, h * HEAD_W:(h + 1) * HEAD_W]
            sq = zs * zs
            low = lax.broadcasted_iota(jnp.int32, sq.shape, 1) < DA_HD
            s_all = jnp.sum(sq, axis=1, keepdims=True)
            s_low = jnp.sum(jnp.where(low, sq, 0.0), axis=1, keepdims=True)
            ms = jnp.where(low, s_low, s_all - s_low) * (1.0 / DA_HD)
            outs.append(zs * lax.rsqrt(ms + EPS))
        return jnp.concatenate(outs, axis=1) * g

    def rotary(z, rs):
        cos, sin = cos_ref[rs, :], sin_ref[rs, :]
        outs = []
        for h in range(HEADS):
            zs = z[:, h * HEAD_W:(h + 1) * HEAD_W]
            outs.append(zs * cos + pltpu.roll(zs, HEAD_W // 2, 1) * sin)
        return jnp.concatenate(outs, axis=1)

    def by_head(z, rs, out32_ref):
        for h in range(HEADS):
            rows = pl.ds(rs.start * HEADS + h, rs.stop - rs.start, stride=HEADS)
            out32_ref[rows, :] = z[:, h * HEAD_W:(h + 1) * HEAD_W]

    def put_q(z, rs):
        q_ref[rs, :] = (group_norm(z, gq_ref[...]) * (DA_HD ** -0.5 * LOG2E)).astype(BF16)

    def put_k(z, rs):
        k = group_norm(z, gk_ref[...])
        kb_ref[rs, :] = k.astype(BF16)
        by_head(k, rs, k32_ref)

    def put_v(z, rs):
        vb_ref[rs, :] = z.astype(BF16)
        by_head(z, rs, v32_ref)

    def put_qr(z, rs):
        qr_ref[rs, :] = rotary(z, rs).astype(BF16)

    def put_kr(z, rs):
        kr_ref[rs, :] = (rotary(z, rs) * RT_KD ** -0.5).astype(BF16)

    def put_vr(z, rs):
        vr_ref[rs, :] = z.astype(BF16)

    def put_sg(z, rs):
        sg_ref[rs, :] = (z * jax.nn.sigmoid(z)).astype(BF16)

    def put_sd(z, rs):
        sd_ref[rs, :] = jax.nn.sigmoid(z).astype(BF16)

    def put_sr(z, rs):
        sr_ref[rs, :] = jax.nn.sigmoid(z).astype(BF16)

    puts = [put_q, put_k, put_v, put_qr, put_kr, put_vr, put_sg, put_sd, put_sr]
    cols = [0, SEG, 2 * SEG, 3 * SEG, 4 * SEG, 5 * SEG, 6 * SEG, 7 * SEG, 7 * SEG + D_MODEL, D_IN]
    z_next = [seg(k, cols[0], cols[1]) for k in range(nparts)]
    for i, put in enumerate(puts):
        for k, rs in enumerate(parts):
            z = z_next[k]
            if i + 1 < len(puts):
                z_next[k] = seg(k, cols[i + 1], cols[i + 2])
            put(z, rs)


def _inproj(x2, cos_t, sin_t, gmix, w_in, gq, gk, tm):
    n = x2.shape[0]
    nt = cos_t.shape[0] // tm
    row = lambda w: pl.BlockSpec((tm, w), lambda i: (i, 0))
    tab = pl.BlockSpec((tm, HEAD_W), lambda i: (i % nt, 0))
    out_w = [(SEG, BF16), (None, F32), (SEG, BF16), (None, F32), (SEG, BF16), (SEG, BF16), (SEG, BF16),
             (SEG, BF16), (SEG, BF16), (D_MODEL, BF16), (D_MODEL, BF16)]
    by_head = pl.BlockSpec((tm * HEADS, HEAD_W), lambda i: (i, 0))
    return pl.pallas_call(
        _inproj_kernel,
        grid=(n // tm,),
        in_specs=[row(D_MODEL), tab, tab, _const_spec((1, D_MODEL)), _const_spec((D_MODEL, D_IN)),
                  _const_spec((1, SEG)), _const_spec((1, SEG))],
        out_specs=[by_head if w is None else row(w) for w, _ in out_w],
        out_shape=[jax.ShapeDtypeStruct((n * HEADS, HEAD_W) if w is None else (n, w), dt) for w, dt in out_w],
        compiler_params=_cparams("arbitrary"),
        name="inproj",
    )(x2, cos_t, sin_t, gmix, w_in, gq, gk)


def _split_maps(q):
    lane = lax.broadcasted_iota(jnp.int32, q.shape, 1)
    zero = jnp.zeros_like(q)
    return jnp.concatenate([jnp.where(lane < DA_HD, q, zero), jnp.where(lane >= DA_HD, q, zero)], axis=0)


def _flash_init(m_scr, l_scr, acc_scr):
    m_scr[...] = jnp.full(m_scr.shape, MASK_VALUE, F32)
    l_scr[...] = jnp.zeros(l_scr.shape, F32)
    acc_scr[...] = jnp.zeros(acc_scr.shape, F32)


def _flash_step_units(qqs, ks, vs, biases, m_scr, l_scr, acc_scr):
    nt = (((1,), (1,)), ((), ()))
    units = range(len(qqs))
    ss = [lax.dot_general(qq, k, nt, preferred_element_type=F32) for qq, k in zip(qqs, ks)]
    ss = [s if b is None else s + b for s, b in zip(ss, biases)]
    m_prev = [m_scr[u] for u in units]
    l_prev = [l_scr[u] for u in units]
    acc_prev = [acc_scr[u] for u in units]
    m_new = [jnp.maximum(mp, jnp.max(s, axis=1, keepdims=True)) for mp, s in zip(m_prev, ss)]
    alpha = [jnp.exp2(mp - mn) for mp, mn in zip(m_prev, m_new)]
    ps = [jnp.exp2(s - mn[:, :1]) for s, mn in zip(ss, m_new)]
    pv = [jnp.dot(p.astype(BF16), v, preferred_element_type=F32) for p, v in zip(ps, vs)]
    for u in units:
        l_scr[u] = alpha[u] * l_prev[u] + jnp.sum(ps[u], axis=1, keepdims=True)
        acc_scr[u] = alpha[u] * acc_prev[u] + pv[u]
        m_scr[u] = m_new[u]


FLASH_STRIP = 64


def _flash_step_staged(qqs, ks, vs, bias_fn, first, m_scr, l_scr, acc_scr, s_scr, p_scr, a_scr):
    rows = s_scr.shape[1]
    tk = ks[0].shape[0]
    for h in range(HEADS):
        s_scr[h, :, :tk] = lax.dot_general(qqs[h], ks[h], (((1,), (1,)), ((), ())), preferred_element_type=F32)
        for r0 in range(0, rows, FLASH_STRIP):
            rs = slice(r0, r0 + FLASH_STRIP)
            s = s_scr[h, rs, :tk]
            if bias_fn is not None:
                s = s + bias_fn(h, r0, FLASH_STRIP)
            m_new = jnp.max(s, axis=1, keepdims=True)
            if first:
                m_new = jnp.broadcast_to(m_new, (FLASH_STRIP, HEAD_W))
            else:
                m_prev = m_scr[h, rs, :]
                m_new = jnp.maximum(m_prev, m_new)
                a_scr[h, rs, :] = jnp.exp2(m_prev - m_new)
            m_scr[h, rs, :] = m_new
            p_scr[h, rs, :tk] = jnp.exp2(s - jnp.concatenate([m_new] * (tk // HEAD_W), axis=1)).astype(BF16)
        v = vs[h]
        pv = jnp.dot(p_scr[h, :, :tk], jnp.concatenate([v, jnp.ones_like(v)], axis=1),
                     preferred_element_type=F32)
        if first:
            acc_scr[h] = pv[:, :HEAD_W]
            l_scr[h] = pv[:, HEAD_W:]
        else:
            alpha = a_scr[h]
            acc_scr[h] = alpha * acc_scr[h] + pv[:, :HEAD_W]
            l_scr[h] = alpha * l_scr[h] + pv[:, HEAD_W:]


def _lam(lamv_ref, lam_init):
    lv = lamv_ref[...]
    s1 = jnp.sum(lv[0:1] * lv[1:2], axis=1, keepdims=True)
    s2 = jnp.sum(lv[2:3] * lv[3:4], axis=1, keepdims=True)
    return jnp.exp(s1) - jnp.exp(s2) + lam_init


def _flash_finish(t, lam, lam_init, gda_ref, l_scr, acc_scr):
    o = acc_scr[...] / l_scr[...]
    o = o[:t] - lam * o[t:]
    return _rms(o, gda_ref[...]) * (1.0 - lam_init)


def _retention_step(qr, kr, vr, s_prev, dmat, dq, dk, dc):
    att = lax.dot_general(qr, kr, (((1,), (1,)), ((), ())), preferred_element_type=F32) * dmat
    o = jnp.dot(att.astype(BF16), vr, preferred_element_type=F32)
    o = o + jnp.dot(qr, s_prev.astype(BF16), preferred_element_type=F32) * dq
    kdec = (kr.astype(F32) * dk).astype(BF16)
    s_new = s_prev * dc + lax.dot_general(kdec, vr, (((0,), (0,)), ((), ())), preferred_element_type=F32)
    return o, s_new


def _decay_tables(c):
    log_g = jnp.log1p(-jnp.power(2.0, -5.0 - jnp.arange(HEADS, dtype=F32)))
    idx = jnp.arange(c, dtype=F32)
    diff = idx[:, None] - idx[None, :]
    dmat = jnp.where(diff >= 0, jnp.exp(log_g[:, None, None] * jnp.maximum(diff, 0.0)), 0.0)
    dq = jnp.broadcast_to(jnp.exp(log_g[:, None] * (idx + 1.0)[None, :])[:, :, None], (HEADS, c, HEAD_W))
    dk = jnp.broadcast_to(jnp.exp(log_g[:, None] * (c - 1.0 - idx)[None, :])[:, :, None], (HEADS, c, HEAD_W))
    dc = jnp.broadcast_to(jnp.exp(log_g * c)[:, None, None], (HEADS, 8, HEAD_W))
    return dmat.astype(F32), dq.astype(F32), dk.astype(F32), dc.astype(F32)


def _head(h):
    return slice(h * HEAD_W, (h + 1) * HEAD_W)


SEQ_GROUP = 2


def _mixer_prompt_kernel(lam_init, q_ref, k_ref, v_ref, bias_ref, qr_ref, kr_ref, vr_ref, sg_ref,
                         dmat_ref, dq_ref, dk_ref, dc_ref, s0_ref, lamv_ref, gda_ref, grt_ref,
                         od_ref, or_ref, sout_ref, m_scr, l_scr, acc_scr, sc_scr, p_scr, a_scr, s_scr):
    qb = pl.program_id(1)
    ng, t = q_ref.shape[0], q_ref.shape[1]
    qq = [[_split_maps(q_ref[g, :, _head(h)]) for h in range(HEADS)] for g in range(ng)]

    def key_step(kb, nblk, bias_cols, first=False):
        off = pl.multiple_of(kb * t, t)
        bias_fn = None if bias_cols is None else (lambda h, r0, n: bias_ref[h, 0, r0 % t:r0 % t + n, bias_cols])
        for g in range(ng):
            _flash_step_staged(qq[g], [k_ref[g, pl.ds(off, nblk * t), _head(h)] for h in range(HEADS)],
                               [v_ref[g, pl.ds(off, nblk * t), _head(h)] for h in range(HEADS)], bias_fn, first,
                               m_scr.at[g], l_scr.at[g], acc_scr.at[g], sc_scr.at[g], p_scr.at[g], a_scr.at[g])

    key_step(qb, 1, slice(t, 2 * t), first=True)

    @pl.when(qb >= 1)
    def _():
        key_step(qb - 1, 1, slice(0, t))

    n_far = jnp.maximum(qb - 1, 0)

    @pl.when(n_far % 2 == 1)
    def _():
        key_step(n_far - 1, 1, None)

    def far_body(i, carry):
        key_step(2 * i, 2, None)
        return carry

    lax.fori_loop(0, n_far // 2, far_body, 0)
    lam = _lam(lamv_ref, lam_init)

    @pl.when(qb == 0)
    def _():
        s_scr[...] = s0_ref[...]


    for g in range(ng):
        for h in range(HEADS):
            od_ref[g, :, _head(h)] = _flash_finish(
                t, lam, lam_init, gda_ref, l_scr.at[g, h], acc_scr.at[g, h]).astype(BF16)
            o, s_new = _retention_step(qr_ref[g, :, _head(h)], kr_ref[g, :, _head(h)], vr_ref[g, :, _head(h)],
                                       s_scr[g, h], dmat_ref[h], dq_ref[h], dk_ref[h], dc_ref[h, 0:1])
            s_scr[g, h] = s_new
            sout_ref[g, h] = s_new
            or_ref[g, :, _head(h)] = (_rms(o, grt_ref[...]) * sg_ref[g, :, _head(h)].astype(F32)).astype(BF16)


def _mixer_prompt(lam_init, b, t, q, kb, vb, bias, qr, kr, vr, sg, s0, lamv, gda, grt):
    tq = ATTN_BLOCK
    nq = t // tq
    ng = SEQ_GROUP if b % SEQ_GROUP == 0 else 1
    tables = _decay_tables(tq)
    by_seq = lambda a: a.reshape(b, t, SEG)
    tile = pl.BlockSpec((ng, tq, SEG), lambda bi, qb: (bi, qb, 0))
    seq = pl.BlockSpec((ng, t, SEG), lambda bi, qb: (bi, 0, 0))
    state = pl.BlockSpec((ng, HEADS, RT_KD, HEAD_W), lambda bi, qb: (bi, 0, 0, 0))
    stat = pltpu.VMEM((ng, HEADS, 2 * tq, HEAD_W), F32)
    od, orr, s_new = pl.pallas_call(
        functools.partial(_mixer_prompt_kernel, lam_init),
        grid=(b // ng, nq),
        in_specs=[tile, seq, seq, _const_spec(bias.shape), tile, tile, tile, tile]
                 + [_const_spec(a.shape) for a in tables]
                 + [state, _const_spec(lamv.shape), _const_spec(gda.shape), _const_spec(grt.shape)],
        out_specs=[tile, tile, state],
        out_shape=[jax.ShapeDtypeStruct((b, t, SEG), BF16), jax.ShapeDtypeStruct((b, t, SEG), BF16),
                   jax.ShapeDtypeStruct((b, HEADS, RT_KD, HEAD_W), F32)],
        scratch_shapes=[stat, stat, stat, pltpu.VMEM((ng, HEADS, 2 * tq, 2 * tq), F32),
                        pltpu.VMEM((ng, HEADS, 2 * tq, 2 * tq), BF16), stat,
                        pltpu.VMEM((ng, HEADS, RT_KD, HEAD_W), F32)],
        compiler_params=_cparams("arbitrary", "arbitrary"),
        name="mixer_prompt",
    )(by_seq(q), by_seq(kb), by_seq(vb), bias, by_seq(qr), by_seq(kr), by_seq(vr), by_seq(sg), *tables,
      s0, lamv, gda, grt)
    return od.reshape(b * t, SEG), orr.reshape(b * t, SEG), s_new


CACHE_CHUNK = 2048


def _mixer_sample_kernel(lam_init, q_ref, ck_ref, cv_ref, kn_ref, vn_ref, bnear_ref, bnew_ref,
                         qr_ref, kr_ref, vr_ref, sg_ref, dmat_ref, dq_ref, dk_ref, dc_ref, s0_ref,
                         lamv_ref, gda_ref, grt_ref, od_ref, or_ref, sout_ref, m_scr, l_scr, acc_scr):
    c = pl.program_id(1)
    last = pl.num_programs(1) - 1
    ng, t = q_ref.shape[0], q_ref.shape[1]
    units = [(g, h) for g in range(ng) for h in range(HEADS)]
    qq = [_split_maps(q_ref[g, :, _head(h)]) for g, h in units]

    @pl.when(c == 0)
    def _():
        _flash_init(m_scr, l_scr, acc_scr)

    def cache_keys(bias_ref):
        rows = [pl.ds(h, CACHE_CHUNK, stride=HEADS) for h in range(HEADS)]
        _flash_step_units(
            qq, [ck_ref[g, rows[h], :].astype(BF16) for g, h in units],
            [cv_ref[g, rows[h], :].astype(BF16) for g, h in units],
            [None if bias_ref is None else jnp.concatenate([bias_ref[h, 0]] * 2, axis=0) for g, h in units],
            m_scr, l_scr, acc_scr)

    @pl.when(c < last)
    def _():
        cache_keys(None)

    @pl.when(c == last)
    def _():
        cache_keys(bnear_ref)
        lam = _lam(lamv_ref, lam_init)
        _flash_step_units(
            qq, [kn_ref[g, :, _head(h)] for g, h in units], [vn_ref[g, :, _head(h)] for g, h in units],
            [jnp.concatenate([bnew_ref[h, 0]] * 2, axis=0) for g, h in units], m_scr, l_scr, acc_scr)
        for u, (g, h) in enumerate(units):
            od_ref[g, :, _head(h)] = _flash_finish(t, lam, lam_init, gda_ref, l_scr.at[u], acc_scr.at[u]).astype(BF16)
            o, s_new = _retention_step(qr_ref[g, :, _head(h)], kr_ref[g, :, _head(h)], vr_ref[g, :, _head(h)],
                                       s0_ref[g, h], dmat_ref[h], dq_ref[h], dk_ref[h], dc_ref[h, 0:1])
            sout_ref[g, h] = s_new
            or_ref[g, :, _head(h)] = (_rms(o, grt_ref[...]) * sg_ref[g, :, _head(h)].astype(F32)).astype(BF16)


def _mixer_sample(lam_init, layer, b, t, past, q, cache_k, cache_v, kn, vn, bnear, bnew, qr, kr, vr, sg, s0,
                  lamv, gda, grt):
    assert past % CACHE_CHUNK == 0
    ng = SEQ_GROUP if b % SEQ_GROUP == 0 else 1
    tables = _decay_tables(t)
    by_seq = lambda a: a.reshape(b, t, SEG)
    tile = pl.BlockSpec((ng, t, SEG), lambda bi, c: (bi, 0, 0))
    cache_k, cache_v = (a.reshape(-1, past * HEADS, HEAD_W) for a in (cache_k, cache_v))
    cache = pl.BlockSpec((ng, CACHE_CHUNK * HEADS, HEAD_W), lambda bi, c: (layer * (b // ng) + bi, c, 0))
    state = pl.BlockSpec((ng, HEADS, RT_KD, HEAD_W), lambda bi, c: (bi, 0, 0, 0))
    consts = [bnear, bnew]
    stat = pltpu.VMEM((ng * HEADS, 2 * t, HEAD_W), F32)
    od, orr, s_new = pl.pallas_call(
        functools.partial(_mixer_sample_kernel, lam_init),
        grid=(b // ng, past // CACHE_CHUNK),
        in_specs=[tile, cache, cache, tile, tile] + [_const_spec(a.shape) for a in consts]
                 + [tile, tile, tile, tile] + [_const_spec(a.shape) for a in tables]
                 + [state, _const_spec(lamv.shape), _const_spec(gda.shape), _const_spec(grt.shape)],
        out_specs=[tile, tile, state],
        out_shape=[jax.ShapeDtypeStruct((b, t, SEG), BF16), jax.ShapeDtypeStruct((b, t, SEG), BF16),
                   jax.ShapeDtypeStruct((b, HEADS, RT_KD, HEAD_W), F32)],
        scratch_shapes=[stat, stat, stat],
        compiler_params=_cparams("arbitrary", "arbitrary"),
        name="mixer_sample",
    )(by_seq(q), cache_k, cache_v, by_seq(kn), by_seq(vn), bnear, bnew, by_seq(qr), by_seq(kr), by_seq(vr),
      by_seq(sg), *tables, s0, lamv, gda, grt)
    return od.reshape(b * t, SEG), orr.reshape(b * t, SEG), s_new


FFN_CHUNK = 1024
CHANNEL_PART = 256
CONV_HALO = CONV_W - 1
CONV_PAD = 8


def _channel_kernel(nseq, rows, tiles_per_seq, x_ref, od_ref, or_ref, sd_ref, sr_ref, p_ref, st_ref,
                    wbd_ref, wbr_ref, wo_ref, gffn_ref, wg_ref, wu_ref, cw_ref, cb_ref, wd_ref,
                    gpe_ref, wpe_ref, wpg_ref, y_ref, cn_ref, gbuf):
    i = pl.program_id(0)
    tm = x_ref.shape[0]
    dot = functools.partial(jnp.dot, preferred_element_type=F32)
    split = tm % CHANNEL_PART == 0 and (rows % CHANNEL_PART == 0 or CHANNEL_PART % rows == 0)
    nparts = tm // CHANNEL_PART if split else 1
    parts = [slice(k * (tm // nparts), (k + 1) * (tm // nparts)) for k in range(nparts)]

    mix_d = [dot(od_ref[rs, :], wbd_ref[...]) for rs in parts]
    mix_r = [dot(or_ref[rs, :], wbr_ref[...]) for rs in parts]
    h1 = [x_ref[rs, :] + dot((sd_ref[rs, :].astype(F32) * md + sr_ref[rs, :].astype(F32) * mr).astype(BF16),
                             wo_ref[...]) for rs, md, mr in zip(parts, mix_d, mix_r)]
    f = [_rms(h, gffn_ref[...]).astype(BF16) for h in h1]
    pe = [dot(p_ref[rs, :].astype(BF16), wpe_ref[...]) for rs in parts]

    halo0 = CONV_PAD - CONV_HALO
    if tiles_per_seq > 1:
        @pl.when(i % tiles_per_seq == 0)
        def _():
            gbuf[halo0:CONV_PAD, :] = st_ref[0]
    stride = rows + CONV_PAD
    chunks = [slice(c0, min(c0 + FFN_CHUNK, D_FF)) for c0 in range(0, D_FF, FFN_CHUNK)]
    up = lambda k, cs: (dot(f[k], wg_ref[:, cs]), dot(f[k], wu_ref[:, cs]))
    nxt = [up(k, chunks[0]) for k in range(nparts)]
    h2 = list(h1)
    for ci, cs in enumerate(chunks):
        cw = cw_ref[:, cs]
        for k, rs in enumerate(parts):
            g, u = nxt[k]
            if ci + 1 < len(chunks):
                nxt[k] = up(k, chunks[ci + 1])
            n = min(rows, rs.stop - rs.start)
            convs = []
            for r0 in range(rs.start, rs.stop, n):
                j, a = r0 // rows, r0 % rows
                base = j * stride + a
                if tiles_per_seq == 1 and a == 0:
                    gbuf[base + halo0:base + CONV_PAD, cs] = st_ref[j, :, cs]
                gj = g[r0 - rs.start:r0 - rs.start + n]
                gbuf[base + CONV_PAD:base + CONV_PAD + n, cs] = gj
                gc = cb_ref[:, cs]
                for tap in range(CONV_HALO):
                    gc = gc + gbuf[base + halo0 + tap:base + halo0 + tap + n, cs] * cw[tap:tap + 1]
                convs.append(gc + gj * cw[CONV_HALO:CONV_W])
                if a + n == rows:
                    last = gbuf[base + n + halo0:base + n + CONV_PAD, cs]
                    cn_ref[j, :, cs] = last
                    if tiles_per_seq > 1:
                        gbuf[halo0:CONV_PAD, cs] = last
            gc = convs[0] if len(convs) == 1 else jnp.concatenate(convs, axis=0)
            h2[k] = h2[k] + dot((jax.nn.gelu(gc) * u).astype(BF16), wd_ref[cs, :])

    for k, rs in enumerate(parts):
        gate = jax.nn.sigmoid(dot(_rms(h2[k], gpe_ref[...]).astype(BF16), wpg_ref[...]))
        y_ref[rs, :] = h2[k] + pe[k] * gate


def _channel(b, t, tm, x2, od, orr, sd, sr, p2, st, w):
    n = b * t
    rows = min(tm, t)
    nseq = tm // rows
    tiles_per_seq = t // rows
    row = lambda wd: pl.BlockSpec((tm, wd), lambda i: (i, 0))
    if tiles_per_seq == 1:
        st_spec = pl.BlockSpec((nseq, CONV_HALO, D_FF), lambda i: (i, 0, 0))
    else:
        st_spec = pl.BlockSpec((1, CONV_HALO, D_FF), lambda i: (i // tiles_per_seq, 0, 0))
    weights = [w["w_bd"], w["w_br"], w["w_o"], w["g_ffn"], w["w_g"], w["w_u"], w["conv_w"], w["conv_b"],
               w["w_d"], w["g_pe"], w["w_pe"], w["w_pg"]]
    return pl.pallas_call(
        functools.partial(_channel_kernel, nseq, rows, tiles_per_seq),
        grid=(n // tm,),
        in_specs=[row(D_MODEL), row(SEG), row(SEG), row(D_MODEL), row(D_MODEL), row(PE_DIM), st_spec]
                 + [_const_spec(a.shape) for a in weights],
        out_specs=[row(D_MODEL), st_spec],
        out_shape=[jax.ShapeDtypeStruct((n, D_MODEL), F32), jax.ShapeDtypeStruct((b, CONV_HALO, D_FF), F32)],
        scratch_shapes=[pltpu.VMEM((nseq * (rows + CONV_PAD), D_FF), F32)],
        compiler_params=_cparams("arbitrary"),
        name="channel",
    )(x2, od, orr, sd, sr, p2, st, *weights)


def _rope_tables(pos):
    half = HEAD_W // 2
    inv = jnp.power(ROPE_BASE, -jnp.arange(half, dtype=F32) / half)
    ang = pos.astype(F32)[:, None] * inv[None, :]
    cos, sin = jnp.cos(ang), jnp.sin(ang)
    return jnp.concatenate([cos, cos], axis=1), jnp.concatenate([-sin, sin], axis=1)


def _prep_weights(i, g_mix, w_in, g_q, g_k, lam_q1, lam_k1, lam_q2, lam_k2, g_da, g_rt, w_bd, w_br, w_o,
                  g_ffn, w_g, w_u, conv_w, conv_b, w_d, g_pe, w_pe, w_pg):
    return dict(
        g_mix=g_mix[i][None], w_in=w_in[i].astype(BF16),
        g_q=jnp.tile(g_q[i], SEG // DA_HD)[None], g_k=jnp.tile(g_k[i], SEG // DA_HD)[None],
        lamv=jnp.stack([lam_q1[i], lam_k1[i], lam_q2[i], lam_k2[i]]),
        g_da=g_da[i][None], g_rt=g_rt[i][None],
        w_bd=w_bd[i].astype(BF16), w_br=w_br[i].astype(BF16), w_o=w_o[i].astype(BF16),
        g_ffn=g_ffn[i][None], w_g=w_g[i].astype(BF16), w_u=w_u[i].astype(BF16),
        conv_w=conv_w[i], conv_b=conv_b[i][None], w_d=w_d[i].astype(BF16),
        g_pe=g_pe[i][None], w_pe=w_pe[i].astype(BF16), w_pg=w_pg[i].astype(BF16))


def _layer(h, pe, layer, cache_k, cache_v, s_ret, s_conv, rel_bias, lam_init, w):
    b, t, _ = h.shape
    n = b * t
    past = 0 if cache_k is None else cache_k.shape[2]
    x2 = h.reshape(n, D_MODEL)
    qpos = past + jnp.arange(t, dtype=jnp.int32)

    tm_in = min(512, n)
    cos_t, sin_t = _rope_tables(qpos)
    if t < tm_in:
        cos_t, sin_t = jnp.tile(cos_t, (tm_in // t, 1)), jnp.tile(sin_t, (tm_in // t, 1))
    q, k32, kb, v32, vb, qr, kr, vr, sg, sd, sr = _inproj(
        x2, cos_t, sin_t, w["g_mix"], w["w_in"], w["g_q"], w["g_k"], tm_in)

    if cache_k is None:
        tq = ATTN_BLOCK
        assert t % tq == 0 and tq % CHUNK == 0 and tq >= MAX_DIST
        bias = _bias_tiles(rel_bias, tq + jnp.arange(tq, dtype=jnp.int32)[None],
                           jnp.arange(2 * tq, dtype=jnp.int32)[None])
        s0 = jnp.zeros((b, HEADS, RT_KD, HEAD_W), F32)
        od, orr, s_new = _mixer_prompt(lam_init, b, t, q, kb, vb, bias, qr, kr, vr, sg, s0,
                                       w["lamv"], w["g_da"], w["g_rt"])
        st = jnp.zeros((b, CONV_HALO, D_FF), F32)
    else:
        assert past % CHUNK == 0 and t <= CHUNK and CACHE_CHUNK >= MAX_DIST
        near = jnp.arange(past - CACHE_CHUNK, past, dtype=jnp.int32)
        bnear = _bias_tiles(rel_bias, qpos[None], near[None])
        bnew = _bias_tiles(rel_bias, qpos[None], qpos[None])
        od, orr, s_new = _mixer_sample(lam_init, layer, b, t, past, q, cache_k, cache_v, kb, vb, bnear, bnew,
                                       qr, kr, vr, sg, s_ret, w["lamv"], w["g_da"], w["g_rt"])
        st = s_conv

    tm_ch = min(512, n)
    y, conv_new = _channel(b, t, tm_ch, x2, od, orr, sd, sr, pe.reshape(n, PE_DIM), st, w)
    return (y.reshape(b, t, D_MODEL), k32.reshape(b, t, HEADS, HEAD_W), v32.reshape(b, t, HEADS, HEAD_W),
            s_new, conv_new)


def kernel(x_prompt, x_sample, p_prompt, p_sample, cache_k, cache_v, state_ret, state_conv, rel_bias, g_mix, w_in, g_q, g_k, lam_q1, lam_k1, lam_q2, lam_k2, g_da, g_rt, w_bd, w_br, w_o, g_ffn, w_g, w_u, conv_w, conv_b, w_d, g_pe, w_pe, w_pg):
    depth = w_in.shape[0]
    hp, hs = x_prompt, x_sample
    outs = [[] for _ in range(8)]
    for i in range(depth):
        lam_init = 0.8 - 0.6 * math.exp(-0.3 * i)
        w = _prep_weights(i, g_mix, w_in, g_q, g_k, lam_q1, lam_k1, lam_q2, lam_k2, g_da, g_rt, w_bd, w_br,
                          w_o, g_ffn, w_g, w_u, conv_w, conv_b, w_d, g_pe, w_pe, w_pg)
        hp, kp, vp, rp, cp = _layer(hp, p_prompt[i], i, None, None, None, None, rel_bias, lam_init, w)
        hs, ks, vs, rs, cs = _layer(hs, p_sample[i], i, cache_k, cache_v, state_ret[i], state_conv[i],
                                    rel_bias, lam_init, w)
        for lst, val in zip(outs, (kp, vp, rp, cp, ks, vs, rs, cs)):
            lst.append(val)
    return (hp, hs) + tuple(jnp.stack(o) for o in outs)
```

```python
import functools
import math

import jax
import jax.numpy as jnp
from jax import lax
from jax.experimental import pallas as pl
from jax.experimental.pallas import tpu as pltpu

D_MODEL = 1024
CHUNK = 64
PE_DIM = 256
HEADS = 4
HEAD_W = 128
DA_HD = 64
RT_KD = 128
D_FF = 2816
CONV_W = 3
N_BUCKETS = 32
MAX_DIST = 128
EPS = 1e-6
ROPE_BASE = 10000.0
SEG = HEADS * HEAD_W
D_IN = 7 * SEG + 2 * D_MODEL

MASK_VALUE = -1e30
V7X_VMEM_LIMIT = 56 * 1024 * 1024
ATTN_BLOCK = 256
F32 = jnp.float32
BF16 = jnp.bfloat16


def _cparams(*sem):
    return pltpu.CompilerParams(dimension_semantics=sem, vmem_limit_bytes=V7X_VMEM_LIMIT)


def _const_spec(shape):
    nd = len(shape)
    return pl.BlockSpec(shape, lambda *_: (0,) * nd, pipeline_mode=pl.Buffered(1))


def _rms(x, g):
    return x * lax.rsqrt(jnp.mean(x * x, axis=-1, keepdims=True) + EPS) * g


def _t5_bucket(rel):
    nb = N_BUCKETS // 2
    ret = jnp.where(rel > 0, nb, 0)
    n = jnp.abs(rel)
    max_exact = nb // 2
    nf = jnp.maximum(n, 1).astype(F32)
    large = max_exact + (jnp.log(nf / max_exact) / math.log(MAX_DIST / max_exact) * (nb - max_exact)).astype(jnp.int32)
    large = jnp.minimum(large, nb - 1)
    return ret + jnp.where(n < max_exact, n, large)


FAR_BUCKET = N_BUCKETS // 2 - 1
LOG2E = math.log2(math.e)


def _bias_kernel(idx_ref, madd_ref, rb_ref, out_ref):
    h = pl.program_id(0)
    idx = idx_ref[...]
    bias = jnp.zeros(idx.shape, F32)
    for b in range(N_BUCKETS):
        bias = jnp.where(idx == b, rb_ref[b, h], bias)
    out_ref[0] = (bias - rb_ref[FAR_BUCKET, h]) * LOG2E + madd_ref[...]


def _bias_tiles(rel_bias, qpos, kpos):
    rel = kpos[:, None, :] - qpos[:, :, None]
    idx = jnp.bitwise_and(_t5_bucket(rel).astype(jnp.int32), N_BUCKETS - 1)
    madd = jnp.where((kpos[:, None, :] // CHUNK) <= (qpos[:, :, None] // CHUNK), 0.0, MASK_VALUE).astype(F32)
    n, r, c = idx.shape
    return pl.pallas_call(
        _bias_kernel,
        grid=(HEADS,),
        in_specs=[pl.BlockSpec((n, r, c), lambda h: (0, 0, 0)),
                  pl.BlockSpec((n, r, c), lambda h: (0, 0, 0)),
                  pl.BlockSpec(memory_space=pltpu.SMEM)],
        out_specs=pl.BlockSpec((1, n, r, c), lambda h: (h, 0, 0, 0)),
        out_shape=jax.ShapeDtypeStruct((HEADS, n, r, c), F32),
        compiler_params=_cparams("arbitrary"),
        name="bias_tiles",
    )(idx, madd, rel_bias)


INPROJ_PART = 256


def _inproj_kernel(x_ref, cos_ref, sin_ref, gmix_ref, w_ref, gq_ref, gk_ref,
                   q_ref, k32_ref, kb_ref, v32_ref, vb_ref, qr_ref, kr_ref, vr_ref, sg_ref, sd_ref, sr_ref):
    tm = x_ref.shape[0]
    nparts = tm // INPROJ_PART if tm % INPROJ_PART == 0 else 1
    parts = [slice(k * (tm // nparts), (k + 1) * (tm // nparts)) for k in range(nparts)]
    a = [_rms(x_ref[rs, :], gmix_ref[...]).astype(BF16) for rs in parts]

    def seg(k, c0, c1):
        return jnp.dot(a[k], w_ref[:, c0:c1], preferred_element_type=F32)

    def group_norm(z, g):
        outs = []
        for h in range(HEADS):
            zs = z[:, h * HEAD_W:(h + 1) * HEAD_W]
            sq = zs * zs
            low = lax.broadcasted_iota(jnp.int32, sq.shape, 1) < DA_HD
            s_all = jnp.sum(sq, axis=1, keepdims=True)
            s_low = jnp.sum(jnp.where(low, sq, 0.0), axis=1, keepdims=True)
            ms = jnp.where(low, s_low, s_all - s_low) * (1.0 / DA_HD)
            outs.append(zs * lax.rsqrt(ms + EPS))
        return jnp.concatenate(outs, axis=1) * g

    def rotary(z, rs):
        cos, sin = cos_ref[rs, :], sin_ref[rs, :]
        outs = []
        for h in range(HEADS):
            zs = z[:, h * HEAD_W:(h + 1) * HEAD_W]
            outs.append(zs * cos + pltpu.roll(zs, HEAD_W // 2, 1) * sin)
        return jnp.concatenate(outs, axis=1)

    def by_head(z, rs, out32_ref):
        for h in range(HEADS):
            rows = pl.ds(rs.start * HEADS + h, rs.stop - rs.start, stride=HEADS)
            out32_ref[rows, :] = z[:, h * HEAD_W:(h + 1) * HEAD_W]

    def put_q(z, rs):
        q_ref[rs, :] = (group_norm(z, gq_ref[...]) * (DA_HD ** -0.5 * LOG2E)).astype(BF16)

    def put_k(z, rs):
        k = group_norm(z, gk_ref[...])
        kb_ref[rs, :] = k.astype(BF16)
        by_head(k, rs, k32_ref)

    def put_v(z, rs):
        vb_ref[rs, :] = z.astype(BF16)
        by_head(z, rs, v32_ref)

    def put_qr(z, rs):
        qr_ref[rs, :] = rotary(z, rs).astype(BF16)

    def put_kr(z, rs):
        kr_ref[rs, :] = (rotary(z, rs) * RT_KD ** -0.5).astype(BF16)

    def put_vr(z, rs):
        vr_ref[rs, :] = z.astype(BF16)

    def put_sg(z, rs):
        sg_ref[rs, :] = (z * jax.nn.sigmoid(z)).astype(BF16)

    def put_sd(z, rs):
        sd_ref[rs, :] = jax.nn.sigmoid(z).astype(BF16)

    def put_sr(z, rs):
        sr_ref[rs, :] = jax.nn.sigmoid(z).astype(BF16)

    puts = [put_q, put_k, put_v, put_qr, put_kr, put_vr, put_sg, put_sd, put_sr]
    cols = [0, SEG, 2 * SEG, 3 * SEG, 4 * SEG, 5 * SEG, 6 * SEG, 7 * SEG, 7 * SEG + D_MODEL, D_IN]
    z_next = [seg(k, cols[0], cols[1]) for k in range(nparts)]
    for i, put in enumerate(puts):
        for k, rs in enumerate(parts):
            z = z_next[k]
            if i + 1 < len(puts):
                z_next[k] = seg(k, cols[i + 1], cols[i + 2])
            put(z, rs)


def _inproj(x2, cos_t, sin_t, gmix, w_in, gq, gk, tm):
    n = x2.shape[0]
    nt = cos_t.shape[0] // tm
    row = lambda w: pl.BlockSpec((tm, w), lambda i: (i, 0))
    tab = pl.BlockSpec((tm, HEAD_W), lambda i: (i % nt, 0))
    out_w = [(SEG, BF16), (None, F32), (SEG, BF16), (None, F32), (SEG, BF16), (SEG, BF16), (SEG, BF16),
             (SEG, BF16), (SEG, BF16), (D_MODEL, BF16), (D_MODEL, BF16)]
    by_head = pl.BlockSpec((tm * HEADS, HEAD_W), lambda i: (i, 0))
    return pl.pallas_call(
        _inproj_kernel,
        grid=(n // tm,),
        in_specs=[row(D_MODEL), tab, tab, _const_spec((1, D_MODEL)), _const_spec((D_MODEL, D_IN)),
                  _const_spec((1, SEG)), _const_spec((1, SEG))],
        out_specs=[by_head if w is None else row(w) for w, _ in out_w],
        out_shape=[jax.ShapeDtypeStruct((n * HEADS, HEAD_W) if w is None else (n, w), dt) for w, dt in out_w],
        compiler_params=_cparams("arbitrary"),
        name="inproj",
    )(x2, cos_t, sin_t, gmix, w_in, gq, gk)


def _split_maps(q):
    lane = lax.broadcasted_iota(jnp.int32, q.shape, 1)
    zero = jnp.zeros_like(q)
    return jnp.concatenate([jnp.where(lane < DA_HD, q, zero), jnp.where(lane >= DA_HD, q, zero)], axis=0)


def _flash_init(m_scr, l_scr, acc_scr):
    m_scr[...] = jnp.full(m_scr.shape, MASK_VALUE, F32)
    l_scr[...] = jnp.zeros(l_scr.shape, F32)
    acc_scr[...] = jnp.zeros(acc_scr.shape, F32)


def _flash_step_units(qqs, ks, vs, biases, m_scr, l_scr, acc_scr):
    nt = (((1,), (1,)), ((), ()))
    units = range(len(qqs))
    ss = [lax.dot_general(qq, k, nt, preferred_element_type=F32) for qq, k in zip(qqs, ks)]
    ss = [s if b is None else s + b for s, b in zip(ss, biases)]
    m_prev = [m_scr[u] for u in units]
    l_prev = [l_scr[u] for u in units]
    acc_prev = [acc_scr[u] for u in units]
    m_new = [jnp.maximum(mp, jnp.max(s, axis=1, keepdims=True)) for mp, s in zip(m_prev, ss)]
    alpha = [jnp.exp2(mp - mn) for mp, mn in zip(m_prev, m_new)]
    ps = [jnp.exp2(s - mn[:, :1]) for s, mn in zip(ss, m_new)]
    pv = [jnp.dot(p.astype(BF16), v, preferred_element_type=F32) for p, v in zip(ps, vs)]
    for u in units:
        l_scr[u] = alpha[u] * l_prev[u] + jnp.sum(ps[u], axis=1, keepdims=True)
        acc_scr[u] = alpha[u] * acc_prev[u] + pv[u]
        m_scr[u] = m_new[u]


FLASH_STRIP = 64


def _flash_step_staged(qqs, ks, vs, bias_fn, first, m_scr, l_scr, acc_scr, s_scr, p_scr, a_scr):
    rows = s_scr.shape[1]
    tk = ks[0].shape[0]
    for h in range(HEADS):
        s_scr[h, :, :tk] = lax.dot_general(qqs[h], ks[h], (((1,), (1,)), ((), ())), preferred_element_type=F32)
        for r0 in range(0, rows, FLASH_STRIP):
            rs = slice(r0, r0 + FLASH_STRIP)
            s = s_scr[h, rs, :tk]
            if bias_fn is not None:
                s = s + bias_fn(h, r0, FLASH_STRIP)
            m_new = jnp.max(s, axis=1, keepdims=True)
            if first:
                m_new = jnp.broadcast_to(m_new, (FLASH_STRIP, HEAD_W))
            else:
                m_prev = m_scr[h, rs, :]
                m_new = jnp.maximum(m_prev, m_new)
                a_scr[h, rs, :] = jnp.exp2(m_prev - m_new)
            m_scr[h, rs, :] = m_new
            p_scr[h, rs, :tk] = jnp.exp2(s - jnp.concatenate([m_new] * (tk // HEAD_W), axis=1)).astype(BF16)
        v = vs[h]
        pv = jnp.dot(p_scr[h, :, :tk], jnp.concatenate([v, jnp.ones_like(v)], axis=1),
                     preferred_element_type=F32)
        if first:
            acc_scr[h] = pv[:, :HEAD_W]
            l_scr[h] = pv[:, HEAD_W:]
        else:
            alpha = a_scr[h]
            acc_scr[h] = alpha * acc_scr[h] + pv[:, :HEAD_W]
            l_scr[h] = alpha * l_scr[h] + pv[:, HEAD_W:]


def _lam(lamv_ref, lam_init):
    lv = lamv_ref[...]
    s1 = jnp.sum(lv[0:1] * lv[1:2], axis=1, keepdims=True)
    s2 = jnp.sum(lv[2:3] * lv[3:4], axis=1, keepdims=True)
    return jnp.exp(s1) - jnp.exp(s2) + lam_init


def _flash_finish(t, lam, lam_init, gda_ref, l_scr, acc_scr):
    o = acc_scr[...] / l_scr[...]
    o = o[:t] - lam * o[t:]
    return _rms(o, gda_ref[...]) * (1.0 - lam_init)


def _retention_step(qr, kr, vr, s_prev, dmat, dq, dk, dc):
    att = lax.dot_general(qr, kr, (((1,), (1,)), ((), ())), preferred_element_type=F32) * dmat
    o = jnp.dot(att.astype(BF16), vr, preferred_element_type=F32)
    o = o + jnp.dot(qr, s_prev.astype(BF16), preferred_element_type=F32) * dq
    kdec = (kr.astype(F32) * dk).astype(BF16)
    s_new = s_prev * dc + lax.dot_general(kdec, vr, (((0,), (0,)), ((), ())), preferred_element_type=F32)
    return o, s_new


def _decay_tables(c):
    log_g = jnp.log1p(-jnp.power(2.0, -5.0 - jnp.arange(HEADS, dtype=F32)))
    idx = jnp.arange(c, dtype=F32)
    diff = idx[:, None] - idx[None, :]
    dmat = jnp.where(diff >= 0, jnp.exp(log_g[:, None, None] * jnp.maximum(diff, 0.0)), 0.0)
    dq = jnp.broadcast_to(jnp.exp(log_g[:, None] * (idx + 1.0)[None, :])[:, :, None], (HEADS, c, HEAD_W))
    dk = jnp.broadcast_to(jnp.exp(log_g[:, None] * (c - 1.0 - idx)[None, :])[:, :, None], (HEADS, c, HEAD_W))
    dc = jnp.broadcast_to(jnp.exp(log_g * c)[:, None, None], (HEADS, 8, HEAD_W))
    return dmat.astype(F32), dq.astype(F32), dk.astype(F32), dc.astype(F32)


def _head(h):
    return slice(h * HEAD_W, (h + 1) * HEAD_W)


SEQ_GROUP = 2


def _mixer_prompt_kernel(lam_init, q_ref, k_ref, v_ref, bias_ref, qr_ref, kr_ref, vr_ref, sg_ref,
                         dmat_ref, dq_ref, dk_ref, dc_ref, s0_ref, lamv_ref, gda_ref, grt_ref,
                         od_ref, or_ref, sout_ref, m_scr, l_scr, acc_scr, sc_scr, p_scr, a_scr, s_scr):
    qb = pl.program_id(1)
    ng, t = q_ref.shape[0], q_ref.shape[1]
    qq = [[_split_maps(q_ref[g, :, _head(h)]) for h in range(HEADS)] for g in range(ng)]

    def key_step(kb, nblk, bias_cols, first=False):
        off = pl.multiple_of(kb * t, t)
        bias_fn = None if bias_cols is None else (lambda h, r0, n: bias_ref[h, 0, r0 % t:r0 % t + n, bias_cols])
        for g in range(ng):
            _flash_step_staged(qq[g], [k_ref[g, pl.ds(off, nblk * t), _head(h)] for h in range(HEADS)],
                               [v_ref[g, pl.ds(off, nblk * t), _head(h)] for h in range(HEADS)], bias_fn, first,
                               m_scr.at[g], l_scr.at[g], acc_scr.at[g], sc_scr.at[g], p_scr.at[g], a_scr.at[g])

    key_step(qb, 1, slice(t, 2 * t), first=True)

    @pl.when(qb >= 1)
    def _():
        key_step(qb - 1, 1, slice(0, t))

    n_far = jnp.maximum(qb - 1, 0)

    @pl.when(n_far % 2 == 1)
    def _():
        key_step(n_far - 1, 1, None)

    @pl.when((n_far // 2) % 2 == 1)
    def _():
        key_step((n_far // 4) * 4, 2, None)

    def far_body(i, carry):
        key_step(4 * i, 2, None)
        key_step(4 * i + 2, 2, None)
        return carry

    lax.fori_loop(0, n_far // 4, far_body, 0)
    lam = _lam(lamv_ref, lam_init)

    @pl.when(qb == 0)
    def _():
        s_scr[...] = s0_ref[...]


    for g in range(ng):
        for h in range(HEADS):
            od_ref[g, :, _head(h)] = _flash_finish(
                t, lam, lam_init, gda_ref, l_scr.at[g, h], acc_scr.at[g, h]).astype(BF16)
            o, s_new = _retention_step(qr_ref[g, :, _head(h)], kr_ref[g, :, _head(h)], vr_ref[g, :, _head(h)],
                                       s_scr[g, h], dmat_ref[h], dq_ref[h], dk_ref[h], dc_ref[h, 0:1])
            s_scr[g, h] = s_new
            sout_ref[g, h] = s_new
            or_ref[g, :, _head(h)] = (_rms(o, grt_ref[...]) * sg_ref[g, :, _head(h)].astype(F32)).astype(BF16)


def _mixer_prompt(lam_init, b, t, q, kb, vb, bias, qr, kr, vr, sg, s0, lamv, gda, grt):
    tq = ATTN_BLOCK
    nq = t // tq
    ng = SEQ_GROUP if b % SEQ_GROUP == 0 else 1
    tables = _decay_tables(tq)
    by_seq = lambda a: a.reshape(b, t, SEG)
    tile = pl.BlockSpec((ng, tq, SEG), lambda bi, qb: (bi, qb, 0))
    seq = pl.BlockSpec((ng, t, SEG), lambda bi, qb: (bi, 0, 0))
    state = pl.BlockSpec((ng, HEADS, RT_KD, HEAD_W), lambda bi, qb: (bi, 0, 0, 0))
    stat = pltpu.VMEM((ng, HEADS, 2 * tq, HEAD_W), F32)
    od, orr, s_new = pl.pallas_call(
        functools.partial(_mixer_prompt_kernel, lam_init),
        grid=(b // ng, nq),
        in_specs=[tile, seq, seq, _const_spec(bias.shape), tile, tile, tile, tile]
                 + [_const_spec(a.shape) for a in tables]
                 + [state, _const_spec(lamv.shape), _const_spec(gda.shape), _const_spec(grt.shape)],
        out_specs=[tile, tile, state],
        out_shape=[jax.ShapeDtypeStruct((b, t, SEG), BF16), jax.ShapeDtypeStruct((b, t, SEG), BF16),
                   jax.ShapeDtypeStruct((b, HEADS, RT_KD, HEAD_W), F32)],
        scratch_shapes=[stat, stat, stat, pltpu.VMEM((ng, HEADS, 2 * tq, 2 * tq), F32),
                        pltpu.VMEM((ng, HEADS, 2 * tq, 2 * tq), BF16), stat,
                        pltpu.VMEM((ng, HEADS, RT_KD, HEAD_W), F32)],
        compiler_params=_cparams("arbitrary", "arbitrary"),
        name="mixer_prompt",
    )(by_seq(q), by_seq(kb), by_seq(vb), bias, by_seq(qr), by_seq(kr), by_seq(vr), by_seq(sg), *tables,
      s0, lamv, gda, grt)
    return od.reshape(b * t, SEG), orr.reshape(b * t, SEG), s_new


CACHE_CHUNK = 2048


def _mixer_sample_kernel(lam_init, q_ref, ck_ref, cv_ref, kn_ref, vn_ref, bnear_ref, bnew_ref,
                         qr_ref, kr_ref, vr_ref, sg_ref, dmat_ref, dq_ref, dk_ref, dc_ref, s0_ref,
                         lamv_ref, gda_ref, grt_ref, od_ref, or_ref, sout_ref, m_scr, l_scr, acc_scr):
    c = pl.program_id(1)
    last = pl.num_programs(1) - 1
    ng, t = q_ref.shape[0], q_ref.shape[1]
    units = [(g, h) for g in range(ng) for h in range(HEADS)]
    qq = [_split_maps(q_ref[g, :, _head(h)]) for g, h in units]

    @pl.when(c == 0)
    def _():
        _flash_init(m_scr, l_scr, acc_scr)

    def cache_keys(bias_ref):
        rows = [pl.ds(h, CACHE_CHUNK, stride=HEADS) for h in range(HEADS)]
        _flash_step_units(
            qq, [ck_ref[g, rows[h], :].astype(BF16) for g, h in units],
            [cv_ref[g, rows[h], :].astype(BF16) for g, h in units],
            [None if bias_ref is None else jnp.concatenate([bias_ref[h, 0]] * 2, axis=0) for g, h in units],
            m_scr, l_scr, acc_scr)

    @pl.when(c < last)
    def _():
        cache_keys(None)

    @pl.when(c == last)
    def _():
        cache_keys(bnear_ref)
        lam = _lam(lamv_ref, lam_init)
        _flash_step_units(
            qq, [kn_ref[g, :, _head(h)] for g, h in units], [vn_ref[g, :, _head(h)] for g, h in units],
            [jnp.concatenate([bnew_ref[h, 0]] * 2, axis=0) for g, h in units], m_scr, l_scr, acc_scr)
        for u, (g, h) in enumerate(units):
            od_ref[g, :, _head(h)] = _flash_finish(t, lam, lam_init, gda_ref, l_scr.at[u], acc_scr.at[u]).astype(BF16)
            o, s_new = _retention_step(qr_ref[g, :, _head(h)], kr_ref[g, :, _head(h)], vr_ref[g, :, _head(h)],
                                       s0_ref[g, h], dmat_ref[h], dq_ref[h], dk_ref[h], dc_ref[h, 0:1])
            sout_ref[g, h] = s_new
            or_ref[g, :, _head(h)] = (_rms(o, grt_ref[...]) * sg_ref[g, :, _head(h)].astype(F32)).astype(BF16)


def _mixer_sample(lam_init, layer, b, t, past, q, cache_k, cache_v, kn, vn, bnear, bnew, qr, kr, vr, sg, s0,
                  lamv, gda, grt):
    assert past % CACHE_CHUNK == 0
    ng = SEQ_GROUP if b % SEQ_GROUP == 0 else 1
    tables = _decay_tables(t)
    by_seq = lambda a: a.reshape(b, t, SEG)
    tile = pl.BlockSpec((ng, t, SEG), lambda bi, c: (bi, 0, 0))
    cache_k, cache_v = (a.reshape(-1, past * HEADS, HEAD_W) for a in (cache_k, cache_v))
    cache = pl.BlockSpec((ng, CACHE_CHUNK * HEADS, HEAD_W), lambda bi, c: (layer * (b // ng) + bi, c, 0))
    state = pl.BlockSpec((ng, HEADS, RT_KD, HEAD_W), lambda bi, c: (bi, 0, 0, 0))
    consts = [bnear, bnew]
    stat = pltpu.VMEM((ng * HEADS, 2 * t, HEAD_W), F32)
    od, orr, s_new = pl.pallas_call(
        functools.partial(_mixer_sample_kernel, lam_init),
        grid=(b // ng, past // CACHE_CHUNK),
        in_specs=[tile, cache, cache, tile, tile] + [_const_spec(a.shape) for a in consts]
                 + [tile, tile, tile, tile] + [_const_spec(a.shape) for a in tables]
                 + [state, _const_spec(lamv.shape), _const_spec(gda.shape), _const_spec(grt.shape)],
        out_specs=[tile, tile, state],
        out_shape=[jax.ShapeDtypeStruct((b, t, SEG), BF16), jax.ShapeDtypeStruct((b, t, SEG), BF16),
                   jax.ShapeDtypeStruct((b, HEADS, RT_KD, HEAD_W), F32)],
        scratch_shapes=[stat, stat, stat],
        compiler_params=_cparams("arbitrary", "arbitrary"),
        name="mixer_sample",
    )(by_seq(q), cache_k, cache_v, by_seq(kn), by_seq(vn), bnear, bnew, by_seq(qr), by_seq(kr), by_seq(vr),
      by_seq(sg), *tables, s0, lamv, gda, grt)
    return od.reshape(b * t, SEG), orr.reshape(b * t, SEG), s_new


FFN_CHUNK = 1024
CHANNEL_PART = 256
CONV_HALO = CONV_W - 1
CONV_PAD = 8


def _channel_kernel(nseq, rows, tiles_per_seq, x_ref, od_ref, or_ref, sd_ref, sr_ref, p_ref, st_ref,
                    wbd_ref, wbr_ref, wo_ref, gffn_ref, wg_ref, wu_ref, cw_ref, cb_ref, wd_ref,
                    gpe_ref, wpe_ref, wpg_ref, y_ref, cn_ref, gbuf):
    i = pl.program_id(0)
    tm = x_ref.shape[0]
    dot = functools.partial(jnp.dot, preferred_element_type=F32)
    split = tm % CHANNEL_PART == 0 and (rows % CHANNEL_PART == 0 or CHANNEL_PART % rows == 0)
    nparts = tm // CHANNEL_PART if split else 1
    parts = [slice(k * (tm // nparts), (k + 1) * (tm // nparts)) for k in range(nparts)]

    mix_d = [dot(od_ref[rs, :], wbd_ref[...]) for rs in parts]
    mix_r = [dot(or_ref[rs, :], wbr_ref[...]) for rs in parts]
    h1 = [x_ref[rs, :] + dot((sd_ref[rs, :].astype(F32) * md + sr_ref[rs, :].astype(F32) * mr).astype(BF16),
                             wo_ref[...]) for rs, md, mr in zip(parts, mix_d, mix_r)]
    f = [_rms(h, gffn_ref[...]).astype(BF16) for h in h1]
    pe = [dot(p_ref[rs, :].astype(BF16), wpe_ref[...]) for rs in parts]

    halo0 = CONV_PAD - CONV_HALO
    if tiles_per_seq > 1:
        @pl.when(i % tiles_per_seq == 0)
        def _():
            gbuf[halo0:CONV_PAD, :] = st_ref[0]
    stride = rows + CONV_PAD
    chunks = [slice(c0, min(c0 + FFN_CHUNK, D_FF)) for c0 in range(0, D_FF, FFN_CHUNK)]
    up = lambda k, cs: (dot(f[k], wg_ref[:, cs]), dot(f[k], wu_ref[:, cs]))
    nxt = [up(k, chunks[0]) for k in range(nparts)]
    h2 = list(h1)
    for ci, cs in enumerate(chunks):
        cw = cw_ref[:, cs]
        for k, rs in enumerate(parts):
            g, u = nxt[k]
            if ci + 1 < len(chunks):
                nxt[k] = up(k, chunks[ci + 1])
            n = min(rows, rs.stop - rs.start)
            convs = []
            for r0 in range(rs.start, rs.stop, n):
                j, a = r0 // rows, r0 % rows
                base = j * stride + a
                if tiles_per_seq == 1 and a == 0:
                    gbuf[base + halo0:base + CONV_PAD, cs] = st_ref[j, :, cs]
                gj = g[r0 - rs.start:r0 - rs.start + n]
                gbuf[base + CONV_PAD:base + CONV_PAD + n, cs] = gj
                gc = cb_ref[:, cs]
                for tap in range(CONV_HALO):
                    gc = gc + gbuf[base + halo0 + tap:base + halo0 + tap + n, cs] * cw[tap:tap + 1]
                convs.append(gc + gj * cw[CONV_HALO:CONV_W])
                if a + n == rows:
                    last = gbuf[base + n + halo0:base + n + CONV_PAD, cs]
                    cn_ref[j, :, cs] = last
                    if tiles_per_seq > 1:
                        gbuf[halo0:CONV_PAD, cs] = last
            gc = convs[0] if len(convs) == 1 else jnp.concatenate(convs, axis=0)
            h2[k] = h2[k] + dot((jax.nn.gelu(gc) * u).astype(BF16), wd_ref[cs, :])

    for k, rs in enumerate(parts):
        gate = jax.nn.sigmoid(dot(_rms(h2[k], gpe_ref[...]).astype(BF16), wpg_ref[...]))
        y_ref[rs, :] = h2[k] + pe[k] * gate


def _channel(b, t, tm, x2, od, orr, sd, sr, p2, st, w):
    n = b * t
    rows = min(tm, t)
    nseq = tm // rows
    tiles_per_seq = t // rows
    row = lambda wd: pl.BlockSpec((tm, wd), lambda i: (i, 0))
    if tiles_per_seq == 1:
        st_spec = pl.BlockSpec((nseq, CONV_HALO, D_FF), lambda i: (i, 0, 0))
    else:
        st_spec = pl.BlockSpec((1, CONV_HALO, D_FF), lambda i: (i // tiles_per_seq, 0, 0))
    weights = [w["w_bd"], w["w_br"], w["w_o"], w["g_ffn"], w["w_g"], w["w_u"], w["conv_w"], w["conv_b"],
               w["w_d"], w["g_pe"], w["w_pe"], w["w_pg"]]
    return pl.pallas_call(
        functools.partial(_channel_kernel, nseq, rows, tiles_per_seq),
        grid=(n // tm,),
        in_specs=[row(D_MODEL), row(SEG), row(SEG), row(D_MODEL), row(D_MODEL), row(PE_DIM), st_spec]
                 + [_const_spec(a.shape) for a in weights],
        out_specs=[row(D_MODEL), st_spec],
        out_shape=[jax.ShapeDtypeStruct((n, D_MODEL), F32), jax.ShapeDtypeStruct((b, CONV_HALO, D_FF), F32)],
        scratch_shapes=[pltpu.VMEM((nseq * (rows + CONV_PAD), D_FF), F32)],
        compiler_params=_cparams("arbitrary"),
        name="channel",
    )(x2, od, orr, sd, sr, p2, st, *weights)


def _rope_tables(pos):
    half = HEAD_W // 2
    inv = jnp.power(ROPE_BASE, -jnp.arange(half, dtype=F32) / half)
    ang = pos.astype(F32)[:, None] * inv[None, :]
    cos, sin = jnp.cos(ang), jnp.sin(ang)
    return jnp.concatenate([cos, cos], axis=1), jnp.concatenate([-sin, sin], axis=1)


def _prep_weights(i, g_mix, w_in, g_q, g_k, lam_q1, lam_k1, lam_q2, lam_k2, g_da, g_rt, w_bd, w_br, w_o,
                  g_ffn, w_g, w_u, conv_w, conv_b, w_d, g_pe, w_pe, w_pg):
    return dict(
        g_mix=g_mix[i][None], w_in=w_in[i].astype(BF16),
        g_q=jnp.tile(g_q[i], SEG // DA_HD)[None], g_k=jnp.tile(g_k[i], SEG // DA_HD)[None],
        lamv=jnp.stack([lam_q1[i], lam_k1[i], lam_q2[i], lam_k2[i]]),
        g_da=g_da[i][None], g_rt=g_rt[i][None],
        w_bd=w_bd[i].astype(BF16), w_br=w_br[i].astype(BF16), w_o=w_o[i].astype(BF16),
        g_ffn=g_ffn[i][None], w_g=w_g[i].astype(BF16), w_u=w_u[i].astype(BF16),
        conv_w=conv_w[i], conv_b=conv_b[i][None], w_d=w_d[i].astype(BF16),
        g_pe=g_pe[i][None], w_pe=w_pe[i].astype(BF16), w_pg=w_pg[i].astype(BF16))


def _layer(h, pe, layer, cache_k, cache_v, s_ret, s_conv, rel_bias, lam_init, w):
    b, t, _ = h.shape
    n = b * t
    past = 0 if cache_k is None else cache_k.shape[2]
    x2 = h.reshape(n, D_MODEL)
    qpos = past + jnp.arange(t, dtype=jnp.int32)

    tm_in = min(1024, n)
    cos_t, sin_t = _rope_tables(qpos)
    if t < tm_in:
        cos_t, sin_t = jnp.tile(cos_t, (tm_in // t, 1)), jnp.tile(sin_t, (tm_in // t, 1))
    q, k32, kb, v32, vb, qr, kr, vr, sg, sd, sr = _inproj(
        x2, cos_t, sin_t, w["g_mix"], w["w_in"], w["g_q"], w["g_k"], tm_in)

    if cache_k is None:
        tq = ATTN_BLOCK
        assert t % tq == 0 and tq % CHUNK == 0 and tq >= MAX_DIST
        bias = _bias_tiles(rel_bias, tq + jnp.arange(tq, dtype=jnp.int32)[None],
                           jnp.arange(2 * tq, dtype=jnp.int32)[None])
        s0 = jnp.zeros((b, HEADS, RT_KD, HEAD_W), F32)
        od, orr, s_new = _mixer_prompt(lam_init, b, t, q, kb, vb, bias, qr, kr, vr, sg, s0,
                                       w["lamv"], w["g_da"], w["g_rt"])
        st = jnp.zeros((b, CONV_HALO, D_FF), F32)
    else:
        assert past % CHUNK == 0 and t <= CHUNK and CACHE_CHUNK >= MAX_DIST
        near = jnp.arange(past - CACHE_CHUNK, past, dtype=jnp.int32)
        bnear = _bias_tiles(rel_bias, qpos[None], near[None])
        bnew = _bias_tiles(rel_bias, qpos[None], qpos[None])
        od, orr, s_new = _mixer_sample(lam_init, layer, b, t, past, q, cache_k, cache_v, kb, vb, bnear, bnew,
                                       qr, kr, vr, sg, s_ret, w["lamv"], w["g_da"], w["g_rt"])
        st = s_conv

    tm_ch = min(512, n)
    y, conv_new = _channel(b, t, tm_ch, x2, od, orr, sd, sr, pe.reshape(n, PE_DIM), st, w)
    return (y.reshape(b, t, D_MODEL), k32.reshape(b, t, HEADS, HEAD_W), v32.reshape(b, t, HEADS, HEAD_W),
            s_new, conv_new)


def kernel(x_prompt, x_sample, p_prompt, p_sample, cache_k, cache_v, state_ret, state_conv, rel_bias, g_mix, w_in, g_q, g_k, lam_q1, lam_k1, lam_q2, lam_k2, g_da, g_rt, w_bd, w_br, w_o, g_ffn, w_g, w_u, conv_w, conv_b, w_d, g_pe, w_pe, w_pg):
    depth = w_in.shape[0]
    hp, hs = x_prompt, x_sample
    outs = [[] for _ in range(8)]
    for i in range(depth):
        lam_init = 0.8 - 0.6 * math.exp(-0.3 * i)
        w = _prep_weights(i, g_mix, w_in, g_q, g_k, lam_q1, lam_k1, lam_q2, lam_k2, g_da, g_rt, w_bd, w_br,
                          w_o, g_ffn, w_g, w_u, conv_w, conv_b, w_d, g_pe, w_pe, w_pg)
        hp, kp, vp, rp, cp = _layer(hp, p_prompt[i], i, None, None, None, None, rel_bias, lam_init, w)
        hs, ks, vs, rs, cs = _layer(hs, p_sample[i], i, cache_k, cache_v, state_ret[i], state_conv[i],
                                    rel_bias, lam_init, w)
        for lst, val in zip(outs, (kp, vp, rp, cp, ks, vs, rs, cs)):
            lst.append(val)
    return (hp, hs) + tuple(jnp.stack(o) for o in outs)
```
